```python
import math
import jax
import jax.numpy as jnp
from jax import lax
import numpy as np

D_MODEL = 1024
BATCH = 16
SEQ = 4096
DEPTH = 2

GRID_W = 64
CTX_LEN = 256
QBLK = 128
EPS = 1e-6
ROPE_THETA = 10000.0

DA_HEADS = 4
DA_HD = 32
DA_W = DA_HEADS * 2 * DA_HD
GA_HEADS = 4
GA_KV = 2
GA_HD = 64
GA_W = GA_HEADS * GA_HD
GM_GROUPS = 4
GM_GW = 64
GM_CHUNK = 128
GM_W = GM_GROUPS * GM_GW
GD_HEADS = 4
GD_HD = 64
GD_W = GD_HEADS * GD_HD
GD_CONV = 5
GD_CHUNK = 64
N_BRANCH = 4
BR_W = 256
N_EXPERTS = 32
TOP_K = 4
D_FF = D_MODEL
SWIGLU_ALPHA = 1.702
SWIGLU_LIMIT = 7.0
EXPERT_BLK = 256

IN_SPLITS = (DA_W, DA_W, DA_W, GA_W, GA_KV * GA_HD, GA_KV * GA_HD, GM_W, GM_W, 3 * GD_W, GD_W, 2 * GD_HEADS, 2 * GD_HEADS, N_BRANCH * D_MODEL)
IN_COLS = sum(IN_SPLITS)

kernel_name = 'hybrid_diffusion_prefix_trunk'


def _rms(x, g):
    xf = x.astype(jnp.float32)
    y = xf * lax.rsqrt(jnp.mean(xf * xf, axis=-1, keepdims=True) + EPS)
    return (y * g.astype(jnp.float32)).astype(x.dtype)


def _l2n(x):
    xf = x.astype(jnp.float32)
    return xf * lax.rsqrt(jnp.sum(xf * xf, axis=-1, keepdims=True) + EPS)


def _modulate(h, shift, scale):
    return h * (1 + scale) + shift


def _adaln(cvec, w_mod, b_mod):
    m = (jax.nn.silu(cvec) @ w_mod + b_mod)[:, None, :]
    return jnp.split(m, 6, axis=-1)


def _heads(t, n, d):
    b, l, _ = t.shape
    return t.reshape(b, l, n, d).transpose(0, 2, 1, 3)


def _unheads(t):
    b, n, l, d = t.shape
    return t.transpose(0, 2, 1, 3).reshape(b, l, n * d)


def _axial_rope_tables(n_tok, head_dim):
    rows = n_tok // GRID_W
    row = jnp.broadcast_to(jnp.arange(rows)[:, None], (rows, GRID_W)).reshape(-1).astype(jnp.float32)
    col = jnp.broadcast_to(jnp.arange(GRID_W)[None, :], (rows, GRID_W)).reshape(-1).astype(jnp.float32)
    quarter = head_dim // 4
    inv = ROPE_THETA ** (-jnp.arange(quarter, dtype=jnp.float32) / quarter)
    ar = row[:, None] * inv
    ac = col[:, None] * inv
    return (jnp.cos(ar), jnp.sin(ar), jnp.cos(ac), jnp.sin(ac))


def _rope1(x, cos, sin):
    x1, x2 = jnp.split(x, 2, axis=-1)
    return jnp.concatenate([x1 * cos - x2 * sin, x2 * cos + x1 * sin], axis=-1)


def _axial_rope(x, tabs):
    cr, sr, cc, sc = tabs
    xr, xcol = jnp.split(x, 2, axis=-1)
    return jnp.concatenate([_rope1(xr, cr, sr), _rope1(xcol, cc, sc)], axis=-1).astype(x.dtype)


def _map_query_blocks(fn, q):
    b, n, l, d = q.shape
    nb = l // QBLK
    qb = jnp.moveaxis(q.reshape(b, n, nb, QBLK, d), 2, 0)
    o = jnp.moveaxis(lax.map(fn, qb), 0, 2)
    return o.reshape(o.shape[0], o.shape[1], l, o.shape[-1])


def _diff_attention(q, k, v, qc, kc, vc, q_g, k_g, lq1, lk1, lq2, lk2, sub_g, lam_init, rope, need_ctx):
    f32 = jnp.float32
    nsub = 2 * DA_HEADS
    q = _axial_rope(_rms(_heads(q, nsub, DA_HD), q_g), rope)
    k = _axial_rope(_rms(_heads(k, nsub, DA_HD), k_g), rope)
    v = _heads(v, DA_HEADS, 2 * DA_HD)
    qc = _rms(_heads(qc, nsub, DA_HD), q_g)
    kc = _rms(_heads(kc, nsub, DA_HD), k_g)
    vc = _heads(vc, DA_HEADS, 2 * DA_HD)
    lam = (jnp.exp(jnp.sum(lq1.astype(f32) * lk1.astype(f32)))
           - jnp.exp(jnp.sum(lq2.astype(f32) * lk2.astype(f32))) + lam_init)
    scale = DA_HD ** -0.5

    def attend(qb, kk, vv):
        s = jnp.einsum('bhqd,bhkd->bhqk', qb, kk).astype(f32) * scale
        p = jax.nn.softmax(s, axis=-1)
        b, _, lq, lk = p.shape
        p = p.reshape(b, DA_HEADS, 2, lq, lk)
        a = (p[:, :, 0] - lam * p[:, :, 1]).astype(vv.dtype)
        return jnp.einsum('bhqk,bhkd->bhqd', a, vv)

    k_all = jnp.concatenate([k, kc], axis=2)
    v_all = jnp.concatenate([v, vc], axis=2)
    o = _map_query_blocks(lambda qb: attend(qb, k_all, v_all), q)
    finish = lambda t: _unheads(_rms(t, sub_g) * (1.0 - lam_init))
    return finish(o), (finish(attend(qc, kc, vc)) if need_ctx else None)


def _gqa(q, k, v, qc, kc, vc, q_g, k_g, rope, need_ctx):
    f32 = jnp.float32
    grp = GA_HEADS // GA_KV
    q = _axial_rope(_rms(_heads(q, GA_HEADS, GA_HD), q_g), rope)
    k = _axial_rope(_rms(_heads(k, GA_KV, GA_HD), k_g), rope)
    v = _heads(v, GA_KV, GA_HD)
    qc = _rms(_heads(qc, GA_HEADS, GA_HD), q_g)
    kc = _rms(_heads(kc, GA_KV, GA_HD), k_g)
    vc = _heads(vc, GA_KV, GA_HD)
    scale = GA_HD ** -0.5

    def attend(qb, kk, vv):
        b, _, lq, d = qb.shape
        qg = qb.reshape(b, GA_KV, grp, lq, d)
        s = jnp.einsum('bkgqd,bksd->bkgqs', qg, kk).astype(f32) * scale
        p = jax.nn.softmax(s, axis=-1).astype(vv.dtype)
        return jnp.einsum('bkgqs,bksd->bkgqd', p, vv).reshape(b, GA_HEADS, lq, d)

    k_all = jnp.concatenate([k, kc], axis=2)
    v_all = jnp.concatenate([v, vc], axis=2)
    o = _map_query_blocks(lambda qb: attend(qb, k_all, v_all), q)
    return _unheads(o), (_unheads(attend(qc, kc, vc)) if need_ctx else None)


def _gmlp(u, v, v_g, w_s, b_s):
    b, l, _ = v.shape
    u = jax.nn.gelu(u, approximate=False)
    v = _rms(jax.nn.gelu(v, approximate=False), v_g)
    vr = v.reshape(b, l // GM_CHUNK, GM_CHUNK, GM_GROUPS, GM_GW)
    mixed = jnp.einsum('gpq,bnqgc->bnpgc', w_s, vr) + b_s.T[None, None, :, :, None]
    return u * mixed.reshape(b, l, GM_W)


def _short_conv(x, w):
    y = lax.conv_general_dilated(x, w[:, None, :], window_strides=(1,),
                                 padding=[(GD_CONV // 2, GD_CONV // 2)],
                                 dimension_numbers=('NWC', 'WIO', 'NWC'),
                                 feature_group_count=x.shape[-1])
    return jax.nn.silu(y)


def _gdn_prep(qkv, a, bb, conv_w, a_log, dt_bias):
    f32 = jnp.float32
    b, l, _ = qkv.shape
    qkv = _short_conv(qkv, conv_w).astype(f32)
    q, k, v = jnp.split(qkv, 3, axis=-1)
    q = (_l2n(q.reshape(b, l, GD_HEADS, GD_HD)) * GD_HD ** -0.5).transpose(0, 2, 1, 3)
    k = _l2n(k.reshape(b, l, GD_HEADS, GD_HD)).transpose(0, 2, 1, 3)
    v = v.reshape(b, l, GD_HEADS, GD_HD).transpose(0, 2, 1, 3)
    a = a.astype(f32).reshape(b, l, 2, GD_HEADS)
    g = -jnp.exp(a_log.astype(f32)) * jax.nn.softplus(a + dt_bias.astype(f32))
    beta = jax.nn.sigmoid(bb.astype(f32).reshape(b, l, 2, GD_HEADS))
    return q, k, v, g.transpose(2, 0, 3, 1), beta.transpose(2, 0, 3, 1)


def _gated_delta_chunked(q, k, v, g, beta, s0):
    f32 = jnp.float32
    b, h, l, d = q.shape
    n, cs = l // GD_CHUNK, GD_CHUNK
    rs = lambda t: t.reshape((b, h, n, cs) + t.shape[3:])
    q, k, v, g, beta = rs(q), rs(k), rs(v), rs(g), rs(beta)
    gc = jnp.cumsum(g, axis=-1)
    incl = jnp.tril(jnp.ones((cs, cs), dtype=bool))
    strict = jnp.tril(jnp.ones((cs, cs), dtype=bool), -1)
    diff = gc[..., :, None] - gc[..., None, :]
    decay = jnp.where(incl, jnp.exp(jnp.where(incl, diff, 0.0)), 0.0)
    kk = jnp.einsum('bhnid,bhnjd->bhnij', k, k)
    amat = jnp.where(strict, beta[..., :, None] * kk * decay, 0.0) + jnp.eye(cs, dtype=f32)
    rhs = jnp.concatenate([v * beta[..., None], k * (beta * jnp.exp(gc))[..., None]], axis=-1)
    sol = lax.linalg.triangular_solve(amat, rhs, left_side=True, lower=True, unit_diagonal=True)
    u, w = sol[..., :d], sol[..., d:]
    qk = jnp.einsum('bhnid,bhnjd->bhnij', q, k) * decay
    qdec = q * jnp.exp(gc)[..., None]
    kdec = k * jnp.exp(gc[..., -1:] - gc)[..., None]
    glast = jnp.exp(gc[..., -1])
    xs = tuple(jnp.moveaxis(t, 2, 0) for t in (qdec, kdec, u, w, qk, glast))

    def step(s, inp):
        qd, kd, uu, ww, qkm, gl = inp
        vnew = uu - jnp.einsum('bhck,bhkv->bhcv', ww, s)
        o = jnp.einsum('bhck,bhkv->bhcv', qd, s) + jnp.einsum('bhij,bhjv->bhiv', qkm, vnew)
        s = s * gl[..., None, None] + jnp.einsum('bhck,bhcv->bhkv', kd, vnew)
        return s, o

    s_fin, o = lax.scan(step, s0, xs)
    return s_fin, jnp.moveaxis(o, 0, 2).reshape(b, h, l, d)


def _gdn(qkv, gate, a, bb, qkv_c, gate_c, a_c, bb_c, conv_w, a_log, dt_bias, out_g, need_ctx):
    ql, kl, vl, gl, bl = _gdn_prep(qkv, a, bb, conv_w, a_log, dt_bias)
    qc, kc, vc, gcx, bcx = _gdn_prep(qkv_c, a_c, bb_c, conv_w, a_log, dt_bias)
    bsz = qkv.shape[0]
    o_lat = 0.0
    o_ctx = 0.0
    for dirn in range(2):
        if dirn == 1:
            fl = lambda t: jnp.flip(t, axis=2)
        else:
            fl = lambda t: t
        s0 = jnp.zeros((bsz, GD_HEADS, GD_HD, GD_HD), jnp.float32)
        s_c, oc = _gated_delta_chunked(fl(qc), fl(kc), fl(vc), fl(gcx[dirn]), fl(bcx[dirn]), s0)
        _, ol = _gated_delta_chunked(fl(ql), fl(kl), fl(vl), fl(gl[dirn]), fl(bl[dirn]), s_c)
        o_lat = o_lat + fl(ol)
        o_ctx = o_ctx + fl(oc)

    def finish(o, gt):
        b, h, l, d = o.shape
        y = _rms(o.transpose(0, 2, 1, 3), out_g) * jax.nn.silu(gt.astype(jnp.float32).reshape(b, l, h, d))
        return y.reshape(b, l, h * d).astype(gt.dtype)

    return finish(o_lat, gate), (finish(o_ctx, gate_c) if need_ctx else None)


def _merge(outs, gate_pre, b_gate, w_br, w_o):
    b, l, _ = outs[0].shape
    o_cat = jnp.stack(outs, axis=2)
    br = jnp.einsum('blnw,nwd->blnd', o_cat, w_br)
    g = jax.nn.sigmoid(gate_pre.reshape(b, l, N_BRANCH, D_MODEL) + b_gate)
    return jnp.einsum('bld,de->ble', jnp.sum(g * br, axis=2), w_o)


def _mixer_sublayer(h, hc, w_in, b_gate, da_q_g, da_k_g, da_lam_q1, da_lam_k1, da_lam_q2, da_lam_k2, da_sub_g,
                    ga_q_g, ga_k_g, gm_v_g, gm_ws, gm_bs, gd_conv_w, gd_a_log, gd_dt_bias, gd_out_g,
                    w_br, w_o, lam_init, rope_da, rope_ga, need_ctx):
    cuts = np.cumsum(IN_SPLITS)[:-1].tolist()
    (aq, ak, av, gq, gk, gv, mu, mv, dqkv, dgate, da, db, gpre) = jnp.split(h @ w_in, cuts, axis=-1)
    (aqc, akc, avc, gqc, gkc, gvc, muc, mvc, dqkvc, dgatec, dac, dbc, gprec) = jnp.split(hc @ w_in, cuts, axis=-1)
    oa, oac = _diff_attention(aq, ak, av, aqc, akc, avc, da_q_g, da_k_g, da_lam_q1, da_lam_k1,
                              da_lam_q2, da_lam_k2, da_sub_g, lam_init, rope_da, need_ctx)
    ob, obc = _gqa(gq, gk, gv, gqc, gkc, gvc, ga_q_g, ga_k_g, rope_ga, need_ctx)
    om = _gmlp(mu, mv, gm_v_g, gm_ws, gm_bs)
    od, odc = _gdn(dqkv, dgate, da, db, dqkvc, dgatec, dac, dbc, gd_conv_w, gd_a_log, gd_dt_bias, gd_out_g, need_ctx)
    y = _merge([oa, ob, om, od], gpre, b_gate, w_br, w_o)
    if not need_ctx:
        return y, None
    omc = _gmlp(muc, mvc, gm_v_g, gm_ws, gm_bs)
    yc = _merge([oac, obc, omc, odc], gprec, b_gate, w_br, w_o)
    return y, yc


def _moe(tok, w_router, b_router, w_up, b_up, w_down, b_down):
    n_tok, d = tok.shape
    logits = (tok @ w_router + b_router).astype(jnp.float32)
    top_val, top_idx = lax.top_k(logits, TOP_K)
    gates = jax.nn.softmax(top_val, axis=-1)
    n_asg = n_tok * TOP_K
    e_flat = top_idx.reshape(-1)
    tok_flat = jnp.arange(n_asg, dtype=jnp.int32) // TOP_K
    g_flat = gates.reshape(-1)
    order = jnp.argsort(e_flat, stable=True)
    e_sorted = e_flat[order]
    counts = jnp.bincount(e_flat, length=N_EXPERTS)
    padded = (counts + EXPERT_BLK - 1) // EXPERT_BLK * EXPERT_BLK
    start = jnp.cumsum(counts) - counts
    pend = jnp.cumsum(padded)
    pstart = pend - padded
    dest = pstart[e_sorted] + (jnp.arange(n_asg) - start[e_sorted])
    n_blocks = (n_asg + N_EXPERTS * (EXPERT_BLK - 1)) // EXPERT_BLK + 1
    n_rows = n_blocks * EXPERT_BLK
    row_tok = jnp.full((n_rows,), n_tok, jnp.int32).at[dest].set(tok_flat[order])
    row_gate = jnp.zeros((n_rows,), tok.dtype).at[dest].set(g_flat[order].astype(tok.dtype))
    blk_exp = jnp.minimum(jnp.searchsorted(pend, jnp.arange(n_blocks) * EXPERT_BLK, side='right'), N_EXPERTS - 1)
    tok_pad = jnp.concatenate([tok, jnp.zeros((1, d), tok.dtype)], axis=0)
    xs = tok_pad[row_tok].reshape(n_blocks, EXPERT_BLK, d)

    def expert_block(args):
        xb, e = args
        z = xb @ w_up[e] + b_up[e]
        glu, lin = jnp.split(z, 2, axis=-1)
        glu = jnp.minimum(glu, SWIGLU_LIMIT)
        lin = jnp.clip(lin, -SWIGLU_LIMIT, SWIGLU_LIMIT)
        act = glu * jax.nn.sigmoid(SWIGLU_ALPHA * glu) * (lin + 1)
        return act @ w_down[e] + b_down[e]

    ys = lax.map(expert_block, (xs, blk_exp)).reshape(n_rows, d)
    return jax.ops.segment_sum(ys * row_gate[:, None], row_tok, num_segments=n_tok + 1)[:n_tok]


def setup_inputs(seed: int = 0) -> dict:
    key = jax.random.key(seed)
    ks = iter(jax.random.split(key, 48))
    f32 = jnp.float32
    nrm = lambda shape, s: jax.random.normal(next(ks), shape, f32) * s
    gain = lambda shape: 1.0 + nrm(shape, 0.02)
    L, D, E, F = DEPTH, D_MODEL, N_EXPERTS, D_FF
    dt = jnp.exp(jax.random.uniform(next(ks), (L, 2, GD_HEADS), f32, math.log(1e-3), math.log(1e-1)))
    return {
        'x': nrm((BATCH, SEQ, D), 1.0),
        'c': nrm((BATCH, D), 1.0),
        'ctx': nrm((BATCH, CTX_LEN, D), 1.0),
        'c_ctx': nrm((D,), 1.0),
        'w_mod': nrm((L, D, 6 * D), D ** -0.5),
        'b_mod': nrm((L, 6 * D), 0.02),
        'norm1_g': gain((L, D)),
        'norm2_g': gain((L, D)),
        'w_in': nrm((L, D, IN_COLS), D ** -0.5),
        'b_gate': nrm((L, N_BRANCH, D), 0.02),
        'da_q_g': gain((L, DA_HD)),
        'da_k_g': gain((L, DA_HD)),
        'da_lam_q1': nrm((L, DA_HD), 0.1),
        'da_lam_k1': nrm((L, DA_HD), 0.1),
        'da_lam_q2': nrm((L, DA_HD), 0.1),
        'da_lam_k2': nrm((L, DA_HD), 0.1),
        'da_sub_g': gain((L, 2 * DA_HD)),
        'ga_q_g': gain((L, GA_HD)),
        'ga_k_g': gain((L, GA_HD)),
        'gm_v_g': gain((L, GM_W)),
        'gm_ws': nrm((L, GM_GROUPS, GM_CHUNK, GM_CHUNK), GM_CHUNK ** -0.5),
        'gm_bs': gain((L, GM_GROUPS, GM_CHUNK)),
        'gd_conv_w': nrm((L, GD_CONV, 3 * GD_W), GD_CONV ** -0.5),
        'gd_a_log': jnp.log(jax.random.uniform(next(ks), (L, 2, GD_HEADS), f32, 1.0, 16.0)),
        'gd_dt_bias': dt + jnp.log(-jnp.expm1(-dt)),
        'gd_out_g': gain((L, GD_HD)),
        'w_br': nrm((L, N_BRANCH, BR_W, D), BR_W ** -0.5),
        'w_o': nrm((L, D, D), D ** -0.5),
        'w_router': nrm((L, D, E), D ** -0.5),
        'b_router': nrm((L, E), 0.01),
        'w_up': nrm((L, E, D, 2 * F), D ** -0.5),
        'b_up': nrm((L, E, 2 * F), 0.02),
        'w_down': nrm((L, E, F, D), F ** -0.5),
        'b_down': nrm((L, E, D), 0.02),
    }


def reference(x, c, ctx, c_ctx, w_mod, b_mod, norm1_g, norm2_g, w_in, b_gate,
              da_q_g, da_k_g, da_lam_q1, da_lam_k1, da_lam_q2, da_lam_k2, da_sub_g,
              ga_q_g, ga_k_g, gm_v_g, gm_ws, gm_bs, gd_conv_w, gd_a_log, gd_dt_bias, gd_out_g,
              w_br, w_o, w_router, b_router, w_up, b_up, w_down, b_down):
    bsz, seq, _ = x.shape
    rope_da = _axial_rope_tables(seq, DA_HD)
    rope_ga = _axial_rope_tables(seq, GA_HD)
    xc = ctx
    for l in range(DEPTH):
        need_ctx = l < DEPTH - 1
        lam_init = 0.8 - 0.6 * math.exp(-0.3 * l)
        sh1, sc1, g1, sh2, sc2, g2 = _adaln(c, w_mod[l], b_mod[l])
        sh1c, sc1c, g1c, sh2c, sc2c, g2c = _adaln(c_ctx[None], w_mod[l], b_mod[l])
        h = _modulate(_rms(x, norm1_g[l]), sh1, sc1)
        hc = _modulate(_rms(xc, norm1_g[l]), sh1c, sc1c)
        y, yc = _mixer_sublayer(h, hc, w_in[l], b_gate[l], da_q_g[l], da_k_g[l], da_lam_q1[l], da_lam_k1[l],
                                da_lam_q2[l], da_lam_k2[l], da_sub_g[l], ga_q_g[l], ga_k_g[l], gm_v_g[l],
                                gm_ws[l], gm_bs[l], gd_conv_w[l], gd_a_log[l], gd_dt_bias[l], gd_out_g[l],
                                w_br[l], w_o[l], lam_init, rope_da, rope_ga, need_ctx)
        x = x + g1 * y
        h2 = _modulate(_rms(x, norm2_g[l]), sh2, sc2)
        if need_ctx:
            xc = xc + g1c * yc
            h2c = _modulate(_rms(xc, norm2_g[l]), sh2c, sc2c)
            tok = jnp.concatenate([h2.reshape(-1, D_MODEL), h2c.reshape(-1, D_MODEL)], axis=0)
        else:
            tok = h2.reshape(-1, D_MODEL)
        f = _moe(tok, w_router[l], b_router[l], w_up[l], b_up[l], w_down[l], b_down[l])
        x = x + g2 * f[:bsz * seq].reshape(bsz, seq, D_MODEL)
        if need_ctx:
            xc = xc + g2c * f[bsz * seq:].reshape(xc.shape)
    return x
```

```python
import functools
import math

import numpy as np
import jax
import jax.numpy as jnp
from jax import lax
from jax.experimental import pallas as pl
from jax.experimental.pallas import tpu as pltpu

F32 = jnp.float32
BF16 = jnp.bfloat16
I32 = jnp.int32

D_MODEL = 1024
GRID_W = 64
EPS = 1e-6
ROPE_THETA = 10000.0
DA_HEADS, DA_HD = 4, 32
GA_HEADS, GA_KV, GA_HD = 4, 2, 64
GM_GROUPS, GM_GW, GM_CHUNK = 4, 64, 128
GD_HEADS, GD_HD, GD_CONV, GD_CHUNK = 4, 64, 5, 64
N_BRANCH, BR_W = 4, 256
N_EXPERTS, TOP_K, D_FF = 32, 4, 1024
SWIGLU_ALPHA, SWIGLU_LIMIT = 1.702, 7.0
EXPERT_BLK = 256

LANES = 128
SUBLANES = 8
TM = 256
TQ = 128
HALO = SUBLANES
VMEM_LIMIT = 56 * 1024 * 1024
NEG_BIG = -1e30

C_DAQ, C_DAK, C_DAV, C_GAQ, C_GAK, C_GAV, C_GMU, C_GMV, C_GDQKV, C_GDGATE, C_GDAB, C_END = (
    0, 256, 512, 768, 1024, 1280, 1536, 1792, 2048, 2816, 3072, 3200)


def _params(sem):
    return pltpu.CompilerParams(dimension_semantics=sem, vmem_limit_bytes=VMEM_LIMIT)


def _split2(x):
    hi = x.astype(BF16)
    lo = (x - hi.astype(F32)).astype(BF16)
    return hi, lo


def _split3(x):
    hi = x.astype(BF16)
    r = x - hi.astype(F32)
    mid = r.astype(BF16)
    lo = (r - mid.astype(F32)).astype(BF16)
    return hi, mid, lo


def _dot(a, b):
    return jnp.dot(a, b, preferred_element_type=F32)


def _dot_nt(a, b):
    return lax.dot_general(a, b, (((1,), (1,)), ((), ())), preferred_element_type=F32)


def _dot_tn(a, b):
    return lax.dot_general(a, b, (((0,), (0,)), ((), ())), preferred_element_type=F32)


def _group_sum(x, gg_ref):
    hi, lo = _split2(x)
    return _dot(jnp.concatenate([hi, lo], axis=1), gg_ref[...])


def _sigmoid(x):
    return 1.0 / (1.0 + jnp.exp(-x))


def _silu(x):
    return x * _sigmoid(x)


def _gelu(x):
    return 0.5 * x * (1.0 + lax.erf(x * (1.0 / math.sqrt(2.0))))


def _softplus(x):
    return jnp.maximum(x, 0.0) + jnp.log1p(jnp.exp(-jnp.abs(x)))


def _rms_rows(x, gain):
    return x * lax.rsqrt(jnp.mean(x * x, axis=-1, keepdims=True) + EPS) * gain


def _block_diag(y, mask_ref):
    return jnp.concatenate([y, y, y, y], axis=0) * mask_ref[...]


def _adaln_kernel(c_ref, w_ref, b_ref, o_ref):
    a = _silu(c_ref[...])
    ah, al = _split2(a)
    w = w_ref[...]
    wh, wl = _split2(w)
    o_ref[...] = _dot(ah, wh) + _dot(al, wh) + _dot(ah, wl) + b_ref[...]


def _adaln(cpad, w_mod, b_mod):
    m = cpad.shape[0]
    n = w_mod.shape[1]
    tn = 512
    return pl.pallas_call(
        _adaln_kernel,
        grid=(n // tn,),
        in_specs=[pl.BlockSpec((m, D_MODEL), lambda j: (0, 0)),
                  pl.BlockSpec((D_MODEL, tn), lambda j: (0, j)),
                  pl.BlockSpec((1, tn), lambda j: (0, j))],
        out_specs=pl.BlockSpec((m, tn), lambda j: (0, j)),
        out_shape=jax.ShapeDtypeStruct((m, n), F32),
        compiler_params=_params(("arbitrary",)),
        name="adaln",
    )(cpad, w_mod, b_mod.reshape(1, n))


def _qk_prep(z, gg_ref, nd, gain_ref, cos_ref, sin_ref, scale):
    ss = _group_sum(z * z, gg_ref)
    y = z * lax.rsqrt(ss * (1.0 / nd) + EPS) * gain_ref[...]
    a = y[:, :LANES]
    b = y[:, LANES:]
    c = cos_ref[...]
    s = sin_ref[...]
    out = jnp.concatenate([a * c - b * s, b * c + a * s], axis=1)
    if scale != 1.0:
        out = out * scale
    return out


def _inproj_kernel(x_ref, mod_ref, g1_ref, w_ref, gda_ref, gga_ref, cosa_ref, sina_ref, cosg_ref, sing_ref,
                   daqg_ref, dakg_ref, gaqg_ref, gakg_ref, gmvg_ref, wsp_ref, bsp_ref, bm_ref,
                   daq_o, dak_o, dav_o, gaq_o, gak_o, gav_o, om_o, gdqkv_o, gdgate_o, gdab_o):
    x = x_ref[...]
    mod = mod_ref[...]
    sh = mod[:, 0:D_MODEL]
    sc = mod[:, D_MODEL:2 * D_MODEL]
    hb = (_rms_rows(x, g1_ref[...]) * (1.0 + sc) + sh).astype(BF16)

    def proj(lo, hi):
        return _dot(hb, w_ref[:, lo:hi])

    daq_o[...] = _qk_prep(proj(C_DAQ, C_DAK), gda_ref, DA_HD, daqg_ref, cosa_ref, sina_ref, DA_HD ** -0.5).astype(BF16)
    dak_o[...] = _qk_prep(proj(C_DAK, C_DAV), gda_ref, DA_HD, dakg_ref, cosa_ref, sina_ref, 1.0).astype(BF16)
    dav_o[...] = proj(C_DAV, C_GAQ).astype(BF16)
    gaq_o[...] = _qk_prep(proj(C_GAQ, C_GAK), gga_ref, GA_HD, gaqg_ref, cosg_ref, sing_ref, GA_HD ** -0.5).astype(BF16)
    gak_o[...] = _qk_prep(proj(C_GAK, C_GAV), gga_ref, GA_HD, gakg_ref, cosg_ref, sing_ref, 1.0).astype(BF16)
    gav_o[...] = proj(C_GAV, C_GMU).astype(BF16)

    u = _gelu(proj(C_GMU, C_GMV))
    v = _rms_rows(_gelu(proj(C_GMV, C_GDQKV)), gmvg_ref[...])
    for j in range(TM // GM_CHUNK):
        rows = slice(j * GM_CHUNK, (j + 1) * GM_CHUNK)
        bd = _block_diag(v[rows].astype(BF16), bm_ref)
        mixed = _dot(wsp_ref[...], bd) + bsp_ref[...]
        om_o[rows, :] = (u[rows] * mixed).astype(BF16)

    gdqkv_o[...] = proj(C_GDQKV, C_GDGATE)
    gdgate_o[...] = proj(C_GDGATE, C_GDAB)
    gdab_o[...] = proj(C_GDAB, C_END)


def _inproj(x_all, modsel, g1, wcat, consts, nl_tiles):
    b, ntok, _ = x_all.shape
    nt = ntok // TM
    tok = lambda w: pl.BlockSpec((None, TM, w), lambda i, t: (i, t, 0))
    full = lambda a: pl.BlockSpec(a.shape, lambda i, t: (0,) * a.ndim)
    tab = pl.BlockSpec((TM, LANES), lambda i, t: (t, 0))
    (gda, gga, cosa, sina, cosg, sing, daqg, dakg, gaqg, gakg, gmvg, wsp, bsp, bm512) = consts
    in_specs = [tok(D_MODEL),
                pl.BlockSpec((None, None, 1, 6 * D_MODEL), lambda i, t: (i, t // nl_tiles, 0, 0)),
                full(g1), full(wcat), full(gda), full(gga), tab, tab, tab, tab,
                full(daqg), full(dakg), full(gaqg), full(gakg), full(gmvg), full(wsp), full(bsp), full(bm512)]
    widths = [(256, BF16)] * 7 + [(768, F32), (256, F32), (128, F32)]
    return pl.pallas_call(
        _inproj_kernel,
        grid=(b, nt),
        in_specs=in_specs,
        out_specs=[tok(w) for w, _ in widths],
        out_shape=[jax.ShapeDtypeStruct((b, ntok, w), dt) for w, dt in widths],
        compiler_params=_params(("parallel", "parallel")),
        name="inproj",
    )(x_all, modsel, g1, wcat, gda, gga, cosa, sina, cosg, sing, daqg, dakg, gaqg, gakg, gmvg, wsp, bsp, bm512)


def _attn_kernel(lam_ref, q_ref, k_ref, v_ref, gg_ref, subg_ref, o_ref, *, diff, qlanes, post_scale):
    q = q_ref[...]
    k = k_ref[...]
    v = v_ref[...]
    tq = q.shape[0]
    lane = lax.broadcasted_iota(I32, (1, 2 * LANES), 1)
    qhead = (lane % LANES) // qlanes
    vhead = lane // 64
    lam = lam_ref[0]
    zero = jnp.zeros_like(q)

    def softmax_pair(i0, i1):
        lhs = jnp.concatenate([jnp.where(qhead == i0, q, zero), jnp.where(qhead == i1, q, zero)], axis=0)
        s = _dot_nt(lhs, k)
        m = jnp.max(s, axis=-1, keepdims=True)
        p = jnp.exp(s - m)
        l = jnp.sum(p, axis=-1, keepdims=True)
        return p, 1.0 / l

    def body(h, acc):
        if diff:
            p, rl = softmax_pair(2 * h, 2 * h + 1)
            a = (p[:tq] * rl[:tq] - lam * (p[tq:] * rl[tq:])).astype(BF16)
            o = _dot(a, v)
            return acc + jnp.where(vhead == h, o, 0.0)
        p, rl = softmax_pair(2 * h, 2 * h + 1)
        o = _dot(p.astype(BF16), v) * rl
        return acc + jnp.where(vhead == 2 * h, o[:tq], 0.0) + jnp.where(vhead == 2 * h + 1, o[tq:], 0.0)

    nloop = DA_HEADS if diff else GA_HEADS // 2
    acc = lax.fori_loop(0, nloop, body, jnp.zeros((tq, 2 * LANES), F32))
    if diff:
        ss = _group_sum(acc * acc, gg_ref)
        acc = acc * lax.rsqrt(ss * (1.0 / (2 * DA_HD)) + EPS) * subg_ref[...] * post_scale
    o_ref[...] = acc.astype(BF16)


def _attention(lam, q, k, v, gg64, subg, *, diff, post_scale, q_start, q_len, kv_start, kv_len):
    b = q.shape[0]
    qoff = q_start // TQ
    kvoff = kv_start // kv_len
    kern = functools.partial(_attn_kernel, diff=diff, qlanes=16 if diff else 32, post_scale=post_scale)
    full = lambda a: pl.BlockSpec(a.shape, lambda i, j: (0,) * a.ndim)
    kvspec = pl.BlockSpec((None, kv_len, 256), lambda i, j: (i, kvoff, 0))
    return pl.pallas_call(
        kern,
        grid=(b, q_len // TQ),
        in_specs=[pl.BlockSpec(memory_space=pltpu.SMEM),
                  pl.BlockSpec((None, TQ, 256), lambda i, j: (i, qoff + j, 0)),
                  kvspec, kvspec, full(gg64), full(subg)],
        out_specs=pl.BlockSpec((None, TQ, 256), lambda i, j: (i, j, 0)),
        out_shape=jax.ShapeDtypeStruct((b, q_len, 256), BF16),
        compiler_params=_params(("parallel", "parallel")),
        name="diff_attn" if diff else "gqa_attn",
    )(lam, q, k, v, gg64, subg)


def _gdn_local_kernel(xm_ref, xl_ref, xr_ref, ab_ref, cw_ref, nea_ref, dtb_ref, eg_ref, eb_ref,
                      lmat_ref, lmat2_ref, pmask_ref, imask_ref, gg_ref, bm_ref,
                      u_o, w_o, qk_o, qd_o, kd_o, eg_o,
                      xext, q_s, k_s, v_s, gsrc_s, gexp_s, bexp_s, *, nl_tiles, nt_tiles):
    t = pl.program_id(1)
    d = pl.program_id(2)
    nchunk = TM // GD_CHUNK

    @pl.when(d == 0)
    def _prep():
        left_ok = jnp.logical_and(t != 0, t != nl_tiles)
        right_ok = jnp.logical_and(t != nl_tiles - 1, t != nt_tiles - 1)
        xext[pl.ds(0, HALO), :] = jnp.where(left_ok, xl_ref[...], 0.0)
        xext[pl.ds(HALO, TM), :] = xm_ref[...]
        xext[pl.ds(HALO + TM, HALO), :] = jnp.where(right_ok, xr_ref[...], 0.0)
        y = jnp.zeros((TM, 3 * 256), F32)
        for j in range(GD_CONV):
            y = y + xext[pl.ds(HALO - GD_CONV // 2 + j, TM), :] * cw_ref[pl.ds(j, 1), :]
        y = _silu(y)
        qv = y[:, 0:256]
        kv = y[:, 256:512]
        q_s[...] = qv * lax.rsqrt(_group_sum(qv * qv, gg_ref) + EPS) * (GD_HD ** -0.5)
        k_s[...] = kv * lax.rsqrt(_group_sum(kv * kv, gg_ref) + EPS)
        v_s[...] = y[:, 512:768]
        ab = ab_ref[...]
        lane = lax.broadcasted_iota(I32, (1, LANES), 1)
        gval = nea_ref[...] * _softplus(ab + dtb_ref[...])
        gsrc_s[...] = jnp.where(lane < 2 * GD_HEADS, gval, _sigmoid(ab))

    g1, g2, g3 = _split3(gsrc_s[...])
    src3 = jnp.concatenate([g1, g2, g3], axis=1)
    gexp_s[...] = _dot(src3, eg_ref[...])
    bexp_s[...] = _dot(src3, eb_ref[...])
    incl = pmask_ref[0]
    strict = pmask_ref[1]
    after = pmask_ref[2]
    eye = pmask_ref[3]

    def mm3(xp, yp):
        xh, xl = _split2(xp)
        yh, yl = _split2(yp)
        lhs = jnp.concatenate([xh, xh, xl], axis=1)
        rhs = jnp.concatenate([_block_diag(yh, bm_ref), _block_diag(yl, bm_ref), _block_diag(yh, bm_ref)], axis=0)
        return _dot(lhs, rhs)

    def chunk(c, carry):
        rows = pl.ds(pl.multiple_of(c * GD_CHUNK, GD_CHUNK), GD_CHUNK)
        qc = q_s[rows, :]
        kc = k_s[rows, :]
        vc = v_s[rows, :]
        gch = gexp_s[rows, :]
        bx = bexp_s[rows, :]
        a1, a2, a3 = _split3(gch)
        cums = _dot(lmat_ref[...], jnp.concatenate([a1, a2, a3], axis=0))
        gcx = cums[0:GD_CHUNK]
        rest = cums[GD_CHUNK:2 * GD_CHUNK]
        tot = cums[2 * GD_CHUNK:3 * GD_CHUNK]
        m1, m2, m3 = _split3(gch * after)
        dif = _dot(lmat2_ref[...], jnp.concatenate([m1, m2, m3], axis=0))
        dec = incl * jnp.exp(dif)
        egc = jnp.exp(gcx)
        kb = kc.astype(BF16)
        bdk = _block_diag(kb, bm_ref)
        kq = _dot_nt(jnp.concatenate([kb, qc.astype(BF16)], axis=0), bdk)
        npk = bx * kq[0:GD_CHUNK] * dec * strict
        qk = kq[GD_CHUNK:] * dec
        n8 = npk * imask_ref[0]
        m2 = mm3(n8, n8)
        pinv = eye - n8
        pinv = pinv + mm3(pinv, m2)
        pinv = pinv + mm3(pinv, mm3(m2, m2))
        for lvl in range(1, 4):
            pinv = pinv - mm3(mm3(pinv, npk * imask_ref[lvl]), pinv)
        u = mm3(pinv, vc * bx)
        w = mm3(pinv, kc * bx * egc)
        u_o[rows, :] = u
        w_o[rows, :] = w.astype(BF16)
        qk_o[rows, :] = qk.astype(BF16)
        qd_o[rows, :] = (qc * egc).astype(BF16)
        kd_o[rows, :] = (kc * jnp.exp(rest)).astype(BF16)
        eg_o[pl.ds(pl.multiple_of(c * SUBLANES, SUBLANES), SUBLANES), :] = jnp.exp(tot[0:SUBLANES])
        return carry

    lax.fori_loop(0, nchunk, chunk, 0)


def _gdn_local(gdqkv, gdab, consts, nl_tiles):
    b, ntok, _ = gdqkv.shape
    nt = ntok // TM
    (cw, nea, dtb, eg, eb, lmat, lmat2, pmask, imask, gg64, bm256) = consts
    hpt = TM // HALO
    nhalo = ntok // HALO
    full = lambda a: pl.BlockSpec(a.shape, lambda i, t, d: (0,) * a.ndim)
    dirsel = lambda a: pl.BlockSpec((None,) + a.shape[1:], lambda i, t, d: (d,) + (0,) * (a.ndim - 1))
    out = lambda w: pl.BlockSpec((None, None, TM, w), lambda i, t, d: (d, i, t, 0))
    kern = functools.partial(_gdn_local_kernel, nl_tiles=nl_tiles, nt_tiles=nt)
    nchunk = TM // GD_CHUNK
    return pl.pallas_call(
        kern,
        grid=(b, nt, 2),
        in_specs=[pl.BlockSpec((None, TM, 768), lambda i, t, d: (i, t, 0)),
                  pl.BlockSpec((None, HALO, 768), lambda i, t, d: (i, jnp.maximum(t * hpt - 1, 0), 0)),
                  pl.BlockSpec((None, HALO, 768), lambda i, t, d: (i, jnp.minimum((t + 1) * hpt, nhalo - 1), 0)),
                  pl.BlockSpec((None, TM, LANES), lambda i, t, d: (i, t, 0)),
                  full(cw), full(nea), full(dtb), dirsel(eg), dirsel(eb), dirsel(lmat), dirsel(lmat2), dirsel(pmask),
                  full(imask), full(gg64), full(bm256)],
        out_specs=[out(256), out(256), out(256), out(256), out(256),
                   pl.BlockSpec((None, None, nchunk * SUBLANES, 256), lambda i, t, d: (d, i, t, 0))],
        out_shape=[jax.ShapeDtypeStruct((2, b, ntok, 256), F32)] +
                  [jax.ShapeDtypeStruct((2, b, ntok, 256), BF16)] * 4 +
                  [jax.ShapeDtypeStruct((2, b, nt * nchunk * SUBLANES, 256), F32)],
        scratch_shapes=[pltpu.VMEM((TM + 2 * HALO, 768), F32), pltpu.VMEM((TM, 256), F32),
                        pltpu.VMEM((TM, 256), F32), pltpu.VMEM((TM, 256), F32), pltpu.VMEM((TM, LANES), F32),
                        pltpu.VMEM((TM, 256), F32), pltpu.VMEM((TM, 256), F32)],
        compiler_params=_params(("parallel", "arbitrary", "arbitrary")),
        name="gdn_local",
    )(gdqkv, gdqkv, gdqkv, gdab, cw, nea, dtb, eg, eb, lmat, lmat2, pmask, imask, gg64, bm256)


def _gdn_scan_kernel(u_ref, w_ref, qk_ref, qd_ref, kd_ref, eg_ref, bm_ref, o_ref, s_ref):
    d = pl.program_id(1)
    j = pl.program_id(2)
    nchunk = TM // GD_CHUNK

    @pl.when(j == 0)
    def _init():
        s_ref[...] = jnp.zeros_like(s_ref)

    lane = lax.broadcasted_iota(I32, (1, 256), 1)

    def chunk(i, carry):
        c = jnp.where(d == 0, i, nchunk - 1 - i)
        rows = pl.ds(pl.multiple_of(c * GD_CHUNK, GD_CHUNK), GD_CHUNK)
        s = s_ref[...]
        bds = _block_diag(s.astype(BF16), bm_ref)
        ws_qs = _dot(jnp.concatenate([w_ref[rows, :], qd_ref[rows, :]], axis=0), bds)
        vnew = u_ref[rows, :] - ws_qs[0:GD_CHUNK]
        vb = vnew.astype(BF16)
        o_ref[rows, :] = ws_qs[GD_CHUNK:] + _dot(qk_ref[rows, :], _block_diag(vb, bm_ref))
        z = _dot_tn(kd_ref[rows, :], vb)
        eg = eg_ref[pl.ds(pl.multiple_of(c * SUBLANES, SUBLANES), 1), :]
        snew = s * eg
        for h in range(GD_HEADS):
            snew = snew + jnp.where(lane // GD_HD == h, z[h * GD_HD:(h + 1) * GD_HD], 0.0)
        s_ref[...] = snew
        return carry

    lax.fori_loop(0, nchunk, chunk, 0)


def _gdn_scan(u, w, qk, qd, kd, eg, bm256, nl_tiles):
    _, b, ntok, _ = u.shape
    nt = ntok // TM
    nc_tiles = nt - nl_tiles
    nchunk = TM // GD_CHUNK

    def tile_index(d, j):
        is_ctx = j < nc_tiles
        fwd = jnp.where(is_ctx, nl_tiles + j, j - nc_tiles)
        bwd = jnp.where(is_ctx, nl_tiles + nc_tiles - 1 - j, nl_tiles - 1 - (j - nc_tiles))
        return jnp.where(d == 0, fwd, bwd)

    tile = lambda: pl.BlockSpec((None, None, TM, 256), lambda i, d, j: (d, i, tile_index(d, j), 0))
    return pl.pallas_call(
        _gdn_scan_kernel,
        grid=(b, 2, nt),
        in_specs=[tile(), tile(), tile(), tile(), tile(),
                  pl.BlockSpec((None, None, nchunk * SUBLANES, 256), lambda i, d, j: (d, i, tile_index(d, j), 0)),
                  pl.BlockSpec(bm256.shape, lambda i, d, j: (0, 0))],
        out_specs=tile(),
        out_shape=jax.ShapeDtypeStruct((2, b, ntok, 256), F32),
        scratch_shapes=[pltpu.VMEM((GD_HD, 256), F32)],
        compiler_params=_params(("parallel", "arbitrary", "arbitrary")),
        name="gdn_scan",
    )(u, w, qk, qd, kd, eg, bm256)


def _merge_kernel(x_ref, mod_ref, g1_ref, g2_ref, wg_ref, bg_ref, oa_ref, ob_ref, om_ref, of_ref, ob2_ref, gate_ref,
                  gdog_ref, gg_ref, wbr_ref, wo_ref, wrh_ref, wrl_ref, br_ref,
                  xn_o, h2_o, idx_o, gates_o):
    x = x_ref[...]
    mod = mod_ref[...]
    dm = D_MODEL
    sh1, sc1, gt1 = mod[:, 0:dm], mod[:, dm:2 * dm], mod[:, 2 * dm:3 * dm]
    sh2, sc2 = mod[:, 3 * dm:4 * dm], mod[:, 4 * dm:5 * dm]
    hb = (_rms_rows(x, g1_ref[...]) * (1.0 + sc1) + sh1).astype(BF16)

    o = of_ref[...] + ob2_ref[...]
    ss = _group_sum(o * o, gg_ref)
    od = (o * lax.rsqrt(ss * (1.0 / GD_HD) + EPS) * gdog_ref[...] * _silu(gate_ref[...])).astype(BF16)

    acc = jnp.zeros((TM, dm), F32)
    for n, br in enumerate((oa_ref[...], ob_ref[...], om_ref[...], od)):
        gpre = _dot(hb, wg_ref[:, n * dm:(n + 1) * dm]) + bg_ref[:, n * dm:(n + 1) * dm]
        acc = acc + _sigmoid(gpre) * _dot(br, wbr_ref[n])
    y = _dot(acc.astype(BF16), wo_ref[...])
    xn = x + gt1 * y
    xn_o[...] = xn
    h2 = _rms_rows(xn, g2_ref[...]) * (1.0 + sc2) + sh2
    h2_o[...] = h2

    hh, hl = _split2(h2)
    logits = _dot(hh, wrh_ref[...]) + _dot(hl, wrh_ref[...]) + _dot(hh, wrl_ref[...]) + br_ref[...]
    lane = lax.broadcasted_iota(I32, logits.shape, 1)
    vals = jnp.full(logits.shape, NEG_BIG, F32)
    idxs = jnp.zeros(logits.shape, I32)
    l = logits
    for k in range(TOP_K):
        m = jnp.max(l, axis=-1, keepdims=True)
        i = jnp.min(jnp.where(l == m, lane, LANES), axis=-1, keepdims=True)
        vals = jnp.where(lane == k, m, vals)
        idxs = jnp.where(lane == k, i, idxs)
        l = jnp.where(lane == i, NEG_BIG * 2.0, l)
    e = jnp.exp(vals - jnp.max(vals, axis=-1, keepdims=True))
    e = jnp.where(lane < TOP_K, e, 0.0)
    gates_o[...] = e / jnp.sum(e, axis=-1, keepdims=True)
    idx_o[...] = idxs


def _merge(x_all, modsel, g1, g2, wg, bg, oa, ob, om, o2, gate, gdog, gg64, wbr, wo, wrh, wrl, brp, n_tiles, nl_tiles):
    b = x_all.shape[0]
    ntok_out = n_tiles * TM
    tok = lambda w: pl.BlockSpec((None, TM, w), lambda i, t: (i, t, 0))
    full = lambda a: pl.BlockSpec(a.shape, lambda i, t: (0,) * a.ndim)
    odir = lambda dd: pl.BlockSpec((None, None, TM, 256), lambda i, t: (dd, i, t, 0))
    return pl.pallas_call(
        _merge_kernel,
        grid=(b, n_tiles),
        in_specs=[tok(D_MODEL),
                  pl.BlockSpec((None, None, 1, 6 * D_MODEL), lambda i, t: (i, t // nl_tiles, 0, 0)),
                  full(g1), full(g2), full(wg), full(bg), tok(256), tok(256), tok(256), odir(0), odir(1), tok(256),
                  full(gdog), full(gg64), full(wbr), full(wo), full(wrh), full(wrl), full(brp)],
        out_specs=[tok(D_MODEL), tok(D_MODEL), tok(LANES), tok(LANES)],
        out_shape=[jax.ShapeDtypeStruct((b, ntok_out, D_MODEL), F32), jax.ShapeDtypeStruct((b, ntok_out, D_MODEL), F32),
                   jax.ShapeDtypeStruct((b, ntok_out, LANES), I32), jax.ShapeDtypeStruct((b, ntok_out, LANES), F32)],
        compiler_params=_params(("parallel", "parallel")),
        name="merge",
    )(x_all, modsel, g1, g2, wg, bg, oa, ob, om, o2, o2, gate, gdog, gg64, wbr, wo, wrh, wrl, brp)


def _expert_kernel(bexp_ref, nval_ref, src_cur_ref, src_nxt_ref, dst_ref, tok_hbm, wup_ref, bup_ref, wdn_ref, bdn_ref,
                   out_hbm, xbuf, ybuf, gsem, ssem):
    i = pl.program_id(0)
    nb = pl.num_programs(0)
    slot = i % 2
    nv = nval_ref[i]

    def gather_row(idx_ref, sl, r):
        return pltpu.make_async_copy(tok_hbm.at[pl.ds(idx_ref[0, r], 1), :], xbuf.at[sl, pl.ds(r, 1), :], gsem.at[sl])

    def scatter_row(sl, r):
        return pltpu.make_async_copy(ybuf.at[sl, pl.ds(r, 1), :], out_hbm.at[pl.ds(dst_ref[0, r], 1), :], ssem.at[sl])

    def start_gather(idx_ref, sl, n):
        def issue(r, c):
            gather_row(idx_ref, sl, r).start()
            return c
        lax.fori_loop(0, n, issue, 0)

    def wait_rows(row_copy, sl, n):
        def w(r, c):
            row_copy(sl, 0).wait()
            return c
        lax.fori_loop(0, n, w, 0)

    @pl.when(i == 0)
    def _first():
        xbuf[...] = jnp.zeros_like(xbuf)
        start_gather(src_cur_ref, slot, nv)

    wait_rows(functools.partial(gather_row, src_cur_ref), slot, nv)

    @pl.when(i + 1 < nb)
    def _prefetch():
        start_gather(src_nxt_ref, 1 - slot, nval_ref[jnp.minimum(i + 1, nb - 1)])

    @pl.when(i >= 2)
    def _drain():
        wait_rows(scatter_row, slot, nval_ref[jnp.maximum(i - 2, 0)])

    @pl.when(nv > 0)
    def _compute():
        xb = xbuf[slot].astype(BF16)
        z = _dot(xb, wup_ref[...]) + bup_ref[...]
        glu = jnp.minimum(z[:, :D_FF], SWIGLU_LIMIT)
        lin = jnp.clip(z[:, D_FF:], -SWIGLU_LIMIT, SWIGLU_LIMIT)
        act = glu * _sigmoid(SWIGLU_ALPHA * glu) * (lin + 1.0)
        ybuf[slot] = _dot(act.astype(BF16), wdn_ref[...]) + bdn_ref[...]

        def issue_s(r, c):
            scatter_row(slot, r).start()
            return c
        lax.fori_loop(0, nv, issue_s, 0)

    @pl.when(i == nb - 1)
    def _last():
        wait_rows(scatter_row, slot, nv)

        @pl.when(nb >= 2)
        def _():
            wait_rows(scatter_row, 1 - slot, nval_ref[jnp.maximum(i - 1, 0)])


def _experts(blk_exp, blk_nval, row_src, row_dst, tok, wup, bup, wdn, bdn, out_rows):
    nb = blk_exp.shape[0]
    idx = lambda f: pl.BlockSpec((None, 1, EXPERT_BLK), f, memory_space=pltpu.SMEM)
    gs = pltpu.PrefetchScalarGridSpec(
        num_scalar_prefetch=2,
        grid=(nb,),
        in_specs=[idx(lambda i, be, nv: (i, 0, 0)),
                  idx(lambda i, be, nv: (jnp.minimum(i + 1, nb - 1), 0, 0)),
                  idx(lambda i, be, nv: (i, 0, 0)),
                  pl.BlockSpec(memory_space=pl.ANY),
                  pl.BlockSpec((None, D_MODEL, 2 * D_FF), lambda i, be, nv: (be[i], 0, 0)),
                  pl.BlockSpec((None, 1, 2 * D_FF), lambda i, be, nv: (be[i], 0, 0)),
                  pl.BlockSpec((None, D_FF, D_MODEL), lambda i, be, nv: (be[i], 0, 0)),
                  pl.BlockSpec((None, 1, D_MODEL), lambda i, be, nv: (be[i], 0, 0))],
        out_specs=pl.BlockSpec(memory_space=pl.ANY),
        scratch_shapes=[pltpu.VMEM((2, EXPERT_BLK, D_MODEL), F32), pltpu.VMEM((2, EXPERT_BLK, D_MODEL), F32),
                        pltpu.SemaphoreType.DMA((2,)), pltpu.SemaphoreType.DMA((2,))],
    )
    src3 = row_src.reshape(nb, 1, EXPERT_BLK)
    return pl.pallas_call(
        _expert_kernel,
        grid_spec=gs,
        out_shape=jax.ShapeDtypeStruct((out_rows, D_MODEL), F32),
        compiler_params=_params(("arbitrary",)),
        name="experts",
    )(blk_exp, blk_nval, src3, src3, row_dst.reshape(nb, 1, EXPERT_BLK), tok, wup, bup, wdn, bdn)


def _combine_kernel(x_ref, mod_ref, gates_ref, y0_ref, y1_ref, y2_ref, y3_ref, o_ref):
    g2 = mod_ref[...][:, 5 * D_MODEL:6 * D_MODEL]
    gates = gates_ref[...]
    f = jnp.zeros((TM, D_MODEL), F32)
    for k, y_ref in enumerate((y0_ref, y1_ref, y2_ref, y3_ref)):
        f = f + gates[:, k:k + 1] * y_ref[...]
    o_ref[...] = x_ref[...] + g2 * f


def _combine(xn, modsel, gates, out4, nl_tiles):
    b, ntok, _ = xn.shape
    nt = ntok // TM
    slab = b * nt
    tok = lambda w: pl.BlockSpec((None, TM, w), lambda i, t: (i, t, 0))
    ys = lambda k: pl.BlockSpec((TM, D_MODEL), lambda i, t: (k * slab + i * nt + t, 0))
    return pl.pallas_call(
        _combine_kernel,
        grid=(b, nt),
        in_specs=[tok(D_MODEL),
                  pl.BlockSpec((None, None, 1, 6 * D_MODEL), lambda i, t: (i, t // nl_tiles, 0, 0)),
                  tok(LANES), ys(0), ys(1), ys(2), ys(3)],
        out_specs=tok(D_MODEL),
        out_shape=jax.ShapeDtypeStruct((b, ntok, D_MODEL), F32),
        compiler_params=_params(("parallel", "parallel")),
        name="combine",
    )(xn, modsel, gates, out4, out4, out4, out4)


def _np_consts():
    lane = np.arange(256)
    half, r = lane // 128, lane % 128
    s, j = r // 16, r % 16
    d = np.where(j < 8, j + 8 * half, 16 + (j - 8) + 8 * half)
    perm_da = s * 32 + d
    dperm_da = d
    h, j = r // 32, r % 32
    d = np.where(j < 16, j + 16 * half, 32 + (j - 16) + 16 * half)
    perm_gq = h * 64 + d
    perm_gk = (h // 2) * 64 + d
    dperm_ga = d
    hv, dv = lane // 64, lane % 64
    perm_gv = (hv // 2) * 64 + dv
    grp = lambda key: (key[:, None] == key[None, :]).astype(np.float32)
    gda = grp((lane % 128) // 16)
    gga = grp((lane % 128) // 32)
    g64 = grp(lane // 64)
    stack2 = lambda g: np.concatenate([g, g], axis=0)
    bm256 = (np.arange(256)[:, None] // 64 == lane[None, :] // 64).astype(np.float32)
    bm512 = (np.arange(512)[:, None] // 128 == lane[None, :] // 64).astype(np.float32)
    i = np.arange(64)
    jj = lane % 64
    lmat, lmat2, pmask = [], [], []
    for dirn in range(2):
        if dirn == 0:
            inc = (i[None, :] <= i[:, None]); aft = (i[None, :] > i[:, None])
            p_incl = (i[:, None] >= jj[None, :]); p_strict = (i[:, None] > jj[None, :]); p_after = (i[:, None] > jj[None, :])
        else:
            inc = (i[None, :] >= i[:, None]); aft = (i[None, :] < i[:, None])
            p_incl = (i[:, None] <= jj[None, :]); p_strict = (i[:, None] < jj[None, :]); p_after = (i[:, None] < jj[None, :])
        ones = np.ones((64, 64))
        l1 = np.concatenate([inc, aft, ones], axis=0).astype(np.float32)
        lmat.append(np.concatenate([l1, l1, l1], axis=1))
        l2 = inc.astype(np.float32)
        lmat2.append(np.concatenate([l2, l2, l2], axis=1))
        eye = (i[:, None] == jj[None, :])
        pmask.append(np.stack([p_incl, p_strict, p_after, eye]).astype(np.float32))
    bi, bj = i[:, None], jj[None, :]
    imask = np.stack([bi // 8 == bj // 8,
                      (bi // 16 == bj // 16) & (bi // 8 != bj // 8),
                      (bi // 32 == bj // 32) & (bi // 16 != bj // 16),
                      bi // 32 != bj // 32]).astype(np.float32)
    eg = np.zeros((2, 128, 256), np.float32)
    eb = np.zeros((2, 128, 256), np.float32)
    for dirn in range(2):
        for hh in range(GD_HEADS):
            eg[dirn, dirn * GD_HEADS + hh, hh * 64:(hh + 1) * 64] = 1.0
            eb[dirn, 2 * GD_HEADS + dirn * GD_HEADS + hh, hh * 64:(hh + 1) * 64] = 1.0
    eg = np.concatenate([eg, eg, eg], axis=1)
    eb = np.concatenate([eb, eb, eb], axis=1)
    return dict(perm_da=perm_da, dperm_da=dperm_da, perm_gq=perm_gq, perm_gk=perm_gk, dperm_ga=dperm_ga, perm_gv=perm_gv,
                gda=stack2(gda), gga=stack2(gga), g64=stack2(g64), bm256=bm256, bm512=bm512,
                lmat=np.stack(lmat), lmat2=np.stack(lmat2), pmask=np.stack(pmask), imask=imask, eg=eg, eb=eb)


def _rope_tables(seq, ctx_len, quarter, reps):
    t = jnp.arange(seq)
    row = (t // GRID_W).astype(F32)
    col = (t % GRID_W).astype(F32)
    inv = ROPE_THETA ** (-jnp.arange(quarter, dtype=F32) / quarter)
    ang = jnp.concatenate([row[:, None] * inv, col[:, None] * inv], axis=1)
    ang = jnp.tile(ang, (1, reps))
    cos = jnp.concatenate([jnp.cos(ang), jnp.ones((ctx_len, LANES), F32)], axis=0)
    sin = jnp.concatenate([jnp.sin(ang), jnp.zeros((ctx_len, LANES), F32)], axis=0)
    return cos, sin


def _routing(idx, n_tok):
    n_asg = n_tok * TOP_K
    e_flat = idx.reshape(-1)
    order = jnp.argsort(e_flat, stable=True).astype(I32)
    e_sorted = e_flat[order]
    counts = jnp.bincount(e_flat, length=N_EXPERTS).astype(I32)
    padded = (counts + EXPERT_BLK - 1) // EXPERT_BLK * EXPERT_BLK
    start = jnp.cumsum(counts) - counts
    pend = jnp.cumsum(padded)
    pstart = pend - padded
    dest = pstart[e_sorted] + (jnp.arange(n_asg, dtype=I32) - start[e_sorted])
    n_blocks = (n_asg + N_EXPERTS * (EXPERT_BLK - 1)) // EXPERT_BLK + 1
    n_rows = n_blocks * EXPERT_BLK
    tok_of = order // TOP_K
    slot_of = order % TOP_K
    row_src = jnp.zeros((n_rows,), I32).at[dest].set(tok_of)
    row_dst = jnp.zeros((n_rows,), I32).at[dest].set(slot_of * n_tok + tok_of)
    blk_row0 = jnp.arange(n_blocks, dtype=I32) * EXPERT_BLK
    blk_exp = jnp.minimum(jnp.searchsorted(pend, blk_row0, side='right'), N_EXPERTS - 1).astype(I32)
    blk_nval = jnp.clip(pstart[blk_exp] + counts[blk_exp] - blk_row0, 0, EXPERT_BLK).astype(I32)
    return blk_exp, blk_nval, row_src, row_dst


def kernel(x, c, ctx, c_ctx, w_mod, b_mod, norm1_g, norm2_g, w_in, b_gate, da_q_g, da_k_g, da_lam_q1, da_lam_k1,
           da_lam_q2, da_lam_k2, da_sub_g, ga_q_g, ga_k_g, gm_v_g, gm_ws, gm_bs, gd_conv_w, gd_a_log, gd_dt_bias,
           gd_out_g, w_br, w_o, w_router, b_router, w_up, b_up, w_down, b_down):
    bsz, seq, dm = x.shape
    ctx_len = ctx.shape[1]
    depth = w_mod.shape[0]
    assert dm == D_MODEL and seq % TM == 0 and ctx_len % TM == 0 and seq % ctx_len == 0 and seq % GRID_W == 0
    ntok = seq + ctx_len
    nl_tiles = seq // TM
    nt_tiles = ntok // TM
    cn = _np_consts()
    bfc = lambda a: jnp.asarray(a, BF16)
    gda, gga, g64 = bfc(cn['gda']), bfc(cn['gga']), bfc(cn['g64'])
    bm256, bm512 = bfc(cn['bm256']), bfc(cn['bm512'])
    lmat, lmat2 = bfc(cn['lmat']), bfc(cn['lmat2'])
    pmask = jnp.asarray(cn['pmask'], F32)
    imask = jnp.asarray(cn['imask'], F32)
    eg, eb = bfc(cn['eg']), bfc(cn['eb'])
    cosa, sina = _rope_tables(seq, ctx_len, DA_HD // 4, LANES // (DA_HD // 2))
    cosg, sing = _rope_tables(seq, ctx_len, GA_HD // 4, LANES // (GA_HD // 2))
    cuts = np.cumsum((256, 256, 256, 256, 128, 128, 256, 256, 768, 256, 8, 8))
    mrows = -(-(bsz + 1) // SUBLANES) * SUBLANES
    cpad = jnp.zeros((mrows, dm), F32).at[:bsz].set(c).at[bsz].set(c_ctx)

    x_all = jnp.concatenate([x, ctx], axis=1)
    for l in range(depth):
        need_ctx = l < depth - 1
        lam_init = 0.8 - 0.6 * math.exp(-0.3 * l)
        mod = _adaln(cpad, w_mod[l], b_mod[l])
        modsel = jnp.stack([mod[:bsz], jnp.broadcast_to(mod[bsz], (bsz, 6 * dm))], axis=1).reshape(bsz, 2, 1, 6 * dm)

        wl = w_in[l]
        aq, ak, av, gq, gk, gv, mu, mv, dqkv, dgate, dda, ddb = jnp.split(wl[:, :cuts[-1]], cuts[:-1].tolist(), axis=1)
        wcat = jnp.concatenate([aq[:, cn['perm_da']], ak[:, cn['perm_da']], av, gq[:, cn['perm_gq']],
                                gk[:, cn['perm_gk']], gv[:, cn['perm_gv']], mu, mv, dqkv, dgate, dda, ddb,
                                jnp.zeros((dm, LANES - 4 * GD_HEADS), F32)], axis=1).astype(BF16)
        row = lambda a: a.reshape(1, -1).astype(F32)
        daqg = row(da_q_g[l][cn['dperm_da']])
        dakg = row(da_k_g[l][cn['dperm_da']])
        gaqg = row(ga_q_g[l][cn['dperm_ga']])
        gakg = row(ga_k_g[l][cn['dperm_ga']])
        wsp = gm_ws[l].transpose(1, 0, 2).reshape(GM_CHUNK, GM_GROUPS * GM_CHUNK).astype(BF16)
        bsp = jnp.repeat(gm_bs[l].T, GM_GW, axis=1)
        consts = (gda, gga, cosa, sina, cosg, sing, daqg, dakg, gaqg, gakg, row(gm_v_g[l]), wsp, bsp, bm512)
        daq, dak, dav, gaq, gak, gav, om, gdqkv, gdgate, gdab = _inproj(
            x_all, modsel, row(norm1_g[l]), wcat, consts, nl_tiles)

        lam = (jnp.exp(jnp.sum(da_lam_q1[l] * da_lam_k1[l])) - jnp.exp(jnp.sum(da_lam_q2[l] * da_lam_k2[l]))
               + lam_init).reshape(1).astype(F32)
        subg = row(jnp.tile(da_sub_g[l], DA_HEADS))
        att = functools.partial(_attention, lam, gg64=g64, subg=subg)
        lat = dict(q_start=0, q_len=seq, kv_start=0, kv_len=ntok)
        oa = att(daq, dak, dav, diff=True, post_scale=1.0 - lam_init, **lat)
        ob = att(gaq, gak, gav, diff=False, post_scale=1.0, **lat)
        if need_ctx:
            cx = dict(q_start=seq, q_len=ctx_len, kv_start=seq, kv_len=ctx_len)
            oa = jnp.concatenate([oa, att(daq, dak, dav, diff=True, post_scale=1.0 - lam_init, **cx)], axis=1)
            ob = jnp.concatenate([ob, att(gaq, gak, gav, diff=False, post_scale=1.0, **cx)], axis=1)

        nea = jnp.zeros((1, LANES), F32).at[0, :2 * GD_HEADS].set(-jnp.exp(gd_a_log[l].reshape(-1)))
        dtb = jnp.zeros((1, LANES), F32).at[0, :2 * GD_HEADS].set(gd_dt_bias[l].reshape(-1))
        gconsts = (gd_conv_w[l], nea, dtb, eg, eb, lmat, lmat2, pmask, imask, g64, bm256)
        u, w, qk, qd, kd, egl = _gdn_local(gdqkv, gdab, gconsts, nl_tiles)
        o2 = _gdn_scan(u, w, qk, qd, kd, egl, bm256, nl_tiles)

        n_tiles = nt_tiles if need_ctx else nl_tiles
        wg = wl[:, cuts[-1]:].astype(BF16)
        wr = jnp.zeros((dm, LANES), F32).at[:, :N_EXPERTS].set(w_router[l])
        wrh = wr.astype(BF16)
        wrl = (wr - wrh.astype(F32)).astype(BF16)
        brp = jnp.full((1, LANES), NEG_BIG, F32).at[0, :N_EXPERTS].set(b_router[l])
        xn, h2, idx, gates = _merge(x_all, modsel, row(norm1_g[l]), row(norm2_g[l]), wg, row(b_gate[l]),
                                    oa, ob, om, o2, gdgate, row(jnp.tile(gd_out_g[l], GD_HEADS)), g64,
                                    w_br[l].astype(BF16), w_o[l].astype(BF16), wrh, wrl, brp, n_tiles, nl_tiles)

        n_tok = bsz * n_tiles * TM
        blk_exp, blk_nval, row_src, row_dst = _routing(idx[:, :, :TOP_K], n_tok)
        out4 = _experts(blk_exp, blk_nval, row_src, row_dst, h2.reshape(n_tok, dm), w_up[l].astype(BF16),
                        b_up[l].reshape(N_EXPERTS, 1, 2 * D_FF), w_down[l].astype(BF16),
                        b_down[l].reshape(N_EXPERTS, 1, dm), TOP_K * n_tok)
        x_all = _combine(xn, modsel, gates, out4, nl_tiles)
    return x_all[:, :seq] if x_all.shape[1] != seq else x_all
```

```python
import functools
import math

import numpy as np
import jax
import jax.numpy as jnp
from jax import lax
from jax.experimental import pallas as pl
from jax.experimental.pallas import tpu as pltpu

F32 = jnp.float32
BF16 = jnp.bfloat16
I32 = jnp.int32

D_MODEL = 1024
GRID_W = 64
EPS = 1e-6
ROPE_THETA = 10000.0
DA_HEADS, DA_HD = 4, 32
GA_HEADS, GA_KV, GA_HD = 4, 2, 64
GM_GROUPS, GM_GW, GM_CHUNK = 4, 64, 128
GD_HEADS, GD_HD, GD_CONV, GD_CHUNK = 4, 64, 5, 64
N_BRANCH, BR_W = 4, 256
N_EXPERTS, TOP_K, D_FF = 32, 4, 1024
SWIGLU_ALPHA, SWIGLU_LIMIT = 1.702, 7.0
EXP_SUB = 256
EXP_NSUB = 4
EXP_TILE = EXP_SUB * EXP_NSUB

LANES = 128
SUBLANES = 8
TM = 256
TQ = 128
HALO = SUBLANES
VMEM_LIMIT = 56 * 1024 * 1024
NEG_BIG = -1e30
LOG2E = math.log2(math.e)

C_DAQ, C_DAK, C_DAV, C_GAQ, C_GAK, C_GAV, C_GMU, C_GMV, C_GDQKV, C_GDGATE, C_GDAB, C_END = (
    0, 256, 512, 768, 1024, 1280, 1536, 1792, 2048, 2816, 3072, 3200)


def _params(sem):
    return pltpu.CompilerParams(dimension_semantics=sem, vmem_limit_bytes=VMEM_LIMIT)


def _split2(x):
    hi = x.astype(BF16)
    lo = (x - hi.astype(F32)).astype(BF16)
    return hi, lo


def _split3(x):
    hi = x.astype(BF16)
    r = x - hi.astype(F32)
    mid = r.astype(BF16)
    lo = (r - mid.astype(F32)).astype(BF16)
    return hi, mid, lo


def _dot(a, b):
    return jnp.dot(a, b, preferred_element_type=F32)


def _dot_nt(a, b):
    return lax.dot_general(a, b, (((1,), (1,)), ((), ())), preferred_element_type=F32)


def _dot_tn(a, b):
    return lax.dot_general(a, b, (((0,), (0,)), ((), ())), preferred_element_type=F32)


def _group_sum(x, gg_ref):
    hi, lo = _split2(x)
    return _dot(jnp.concatenate([hi, lo], axis=1), gg_ref[...])


def _sigmoid(x):
    return 1.0 / (1.0 + jnp.exp(-x))


def _silu(x):
    return x * _sigmoid(x)


def _gelu(x):
    return 0.5 * x * (1.0 + lax.erf(x * (1.0 / math.sqrt(2.0))))


def _softplus(x):
    return jnp.maximum(x, 0.0) + jnp.log1p(jnp.exp(-jnp.abs(x)))


def _rms_rows(x, gain):
    return x * lax.rsqrt(jnp.mean(x * x, axis=-1, keepdims=True) + EPS) * gain


def _block_diag(y, mask_ref):
    return jnp.concatenate([y, y, y, y], axis=0) * mask_ref[...]


def _adaln_kernel(c_ref, w_ref, b_ref, o_ref):
    a = _silu(c_ref[...])
    ah, al = _split2(a)
    w = w_ref[...]
    wh, wl = _split2(w)
    o_ref[...] = _dot(ah, wh) + _dot(al, wh) + _dot(ah, wl) + b_ref[...]


def _adaln(cpad, w_mod, b_mod):
    m = cpad.shape[0]
    n = w_mod.shape[1]
    tn = 512
    return pl.pallas_call(
        _adaln_kernel,
        grid=(n // tn,),
        in_specs=[pl.BlockSpec((m, D_MODEL), lambda j: (0, 0)),
                  pl.BlockSpec((D_MODEL, tn), lambda j: (0, j)),
                  pl.BlockSpec((1, tn), lambda j: (0, j))],
        out_specs=pl.BlockSpec((m, tn), lambda j: (0, j)),
        out_shape=jax.ShapeDtypeStruct((m, n), F32),
        compiler_params=_params(("arbitrary",)),
        name="adaln",
    )(cpad, w_mod, b_mod.reshape(1, n))


def _qk_prep(z, gg_ref, nd, gain_ref, cos_ref, sin_ref, scale):
    ss = _group_sum(z * z, gg_ref)
    y = z * lax.rsqrt(ss * (1.0 / nd) + EPS) * gain_ref[...]
    a = y[:, :LANES]
    b = y[:, LANES:]
    c = cos_ref[...]
    s = sin_ref[...]
    out = jnp.concatenate([a * c - b * s, b * c + a * s], axis=1)
    if scale != 1.0:
        out = out * scale
    return out


def _inproj_kernel(x_ref, mod_ref, g1_ref, w_ref, gda_ref, gga_ref, cosa_ref, sina_ref, cosg_ref, sing_ref,
                   daqg_ref, dakg_ref, gaqg_ref, gakg_ref, gmvg_ref, wsp_ref, bsp_ref, bm_ref,
                   daq_o, dak_o, dav_o, gaq_o, gak_o, gav_o, om_o, gdqkv_o, gdgate_o, gdab_o):
    x = x_ref[...]
    mod = mod_ref[...]
    sh = mod[:, 0:D_MODEL]
    sc = mod[:, D_MODEL:2 * D_MODEL]
    hb = (_rms_rows(x, g1_ref[...]) * (1.0 + sc) + sh).astype(BF16)

    def proj(lo, hi):
        return _dot(hb, w_ref[:, lo:hi])

    daq_o[...] = _qk_prep(proj(C_DAQ, C_DAK), gda_ref, DA_HD, daqg_ref, cosa_ref, sina_ref, LOG2E * DA_HD ** -0.5).astype(BF16)
    dak_o[...] = _qk_prep(proj(C_DAK, C_DAV), gda_ref, DA_HD, dakg_ref, cosa_ref, sina_ref, 1.0).astype(BF16)
    dav_o[...] = proj(C_DAV, C_GAQ).astype(BF16)
    gaq_o[...] = _qk_prep(proj(C_GAQ, C_GAK), gga_ref, GA_HD, gaqg_ref, cosg_ref, sing_ref, LOG2E * GA_HD ** -0.5).astype(BF16)
    gak_o[...] = _qk_prep(proj(C_GAK, C_GAV), gga_ref, GA_HD, gakg_ref, cosg_ref, sing_ref, 1.0).astype(BF16)
    gav_o[...] = proj(C_GAV, C_GMU).astype(BF16)

    u = _gelu(proj(C_GMU, C_GMV))
    v = _rms_rows(_gelu(proj(C_GMV, C_GDQKV)), gmvg_ref[...])
    for j in range(TM // GM_CHUNK):
        rows = slice(j * GM_CHUNK, (j + 1) * GM_CHUNK)
        bd = _block_diag(v[rows].astype(BF16), bm_ref)
        mixed = _dot(wsp_ref[...], bd) + bsp_ref[...]
        om_o[rows, :] = (u[rows] * mixed).astype(BF16)

    gdqkv_o[...] = proj(C_GDQKV, C_GDGATE)
    gdgate_o[...] = proj(C_GDGATE, C_GDAB)
    gdab_o[...] = proj(C_GDAB, C_END)


def _inproj(x_all, modsel, g1, wcat, consts, nl_tiles):
    b, ntok, _ = x_all.shape
    nt = ntok // TM
    tok = lambda w: pl.BlockSpec((None, TM, w), lambda i, t: (i, t, 0))
    full = lambda a: pl.BlockSpec(a.shape, lambda i, t: (0,) * a.ndim)
    tab = pl.BlockSpec((TM, LANES), lambda i, t: (t, 0))
    (gda, gga, cosa, sina, cosg, sing, daqg, dakg, gaqg, gakg, gmvg, wsp, bsp, bm512) = consts
    in_specs = [tok(D_MODEL),
                pl.BlockSpec((None, None, 1, 6 * D_MODEL), lambda i, t: (i, t // nl_tiles, 0, 0)),
                full(g1), full(wcat), full(gda), full(gga), tab, tab, tab, tab,
                full(daqg), full(dakg), full(gaqg), full(gakg), full(gmvg), full(wsp), full(bsp), full(bm512)]
    widths = [(256, BF16)] * 7 + [(768, F32), (256, F32), (128, F32)]
    return pl.pallas_call(
        _inproj_kernel,
        grid=(b, nt),
        in_specs=in_specs,
        out_specs=[tok(w) for w, _ in widths],
        out_shape=[jax.ShapeDtypeStruct((b, ntok, w), dt) for w, dt in widths],
        compiler_params=_params(("parallel", "parallel")),
        name="inproj",
    )(x_all, modsel, g1, wcat, gda, gga, cosa, sina, cosg, sing, daqg, dakg, gaqg, gakg, gmvg, wsp, bsp, bm512)


def _attn_kernel(lam_ref, q_ref, k_ref, v_ref, gg_ref, subg_ref, o_ref, *, diff, qlanes, post_scale):
    q = q_ref[...]
    k = k_ref[...]
    v = v_ref[...]
    tq = q.shape[0]
    lane = lax.broadcasted_iota(I32, (1, 2 * LANES), 1)
    qhead = (lane % LANES) // qlanes
    vhead = lane // 64
    lam = lam_ref[0]
    zero = jnp.zeros_like(q)

    def scores(h):
        lhs = jnp.concatenate([jnp.where(qhead == 2 * h, q, zero), jnp.where(qhead == 2 * h + 1, q, zero)], axis=0)
        return _dot_nt(lhs, k)

    def softmax(s):
        m = jnp.max(s, axis=-1, keepdims=True)
        p = jnp.exp2(s - m)
        l = jnp.sum(p, axis=-1, keepdims=True)
        return p, l

    nloop = DA_HEADS if diff else GA_HEADS // 2
    acc = jnp.zeros((tq, 2 * LANES), F32)
    s_next = scores(0)
    for h in range(nloop):
        s_cur = s_next
        if h + 1 < nloop:
            s_next = scores(h + 1)
        p, l = softmax(s_cur)
        if diff:
            a = (p[:tq] - p[tq:] * (lam * l[:tq] / l[tq:])).astype(BF16)
            acc = acc + jnp.where(vhead == h, _dot(a, v) / l[:tq], 0.0)
        else:
            o = _dot(p.astype(BF16), v) / l
            acc = acc + jnp.where(vhead == 2 * h, o[:tq], 0.0) + jnp.where(vhead == 2 * h + 1, o[tq:], 0.0)
    if diff:
        ss = _group_sum(acc * acc, gg_ref)
        acc = acc * lax.rsqrt(ss * (1.0 / (2 * DA_HD)) + EPS) * subg_ref[...] * post_scale
    o_ref[...] = acc.astype(BF16)


def _attention(lam, q, k, v, gg64, subg, *, diff, post_scale, q_start, q_len, kv_start, kv_len):
    b = q.shape[0]
    qoff = q_start // TQ
    kvoff = kv_start // kv_len
    kern = functools.partial(_attn_kernel, diff=diff, qlanes=16 if diff else 32, post_scale=post_scale)
    full = lambda a: pl.BlockSpec(a.shape, lambda i, j: (0,) * a.ndim)
    kvspec = pl.BlockSpec((None, kv_len, 256), lambda i, j: (i, kvoff, 0))
    return pl.pallas_call(
        kern,
        grid=(b, q_len // TQ),
        in_specs=[pl.BlockSpec(memory_space=pltpu.SMEM),
                  pl.BlockSpec((None, TQ, 256), lambda i, j: (i, qoff + j, 0)),
                  kvspec, kvspec, full(gg64), full(subg)],
        out_specs=pl.BlockSpec((None, TQ, 256), lambda i, j: (i, j, 0)),
        out_shape=jax.ShapeDtypeStruct((b, q_len, 256), BF16),
        compiler_params=_params(("parallel", "parallel")),
        name="diff_attn" if diff else "gqa_attn",
    )(lam, q, k, v, gg64, subg)


def _gdn_local_kernel(xm_ref, xl_ref, xr_ref, ab_ref, cw_ref, nea_ref, dtb_ref, eg_ref, eb_ref,
                      lmat_ref, lmat2_ref, pmask_ref, imask_ref, gg_ref, bm_ref,
                      u_o, w_o, qk_o, qd_o, kd_o, eg_o, xext, *, nl_tiles, nt_tiles):
    t = pl.program_id(1)
    nchunk = TM // GD_CHUNK

    left_ok = jnp.logical_and(t != 0, t != nl_tiles)
    right_ok = jnp.logical_and(t != nl_tiles - 1, t != nt_tiles - 1)
    xext[pl.ds(0, HALO), :] = jnp.where(left_ok, xl_ref[...], 0.0)
    xext[pl.ds(HALO, TM), :] = xm_ref[...]
    xext[pl.ds(HALO + TM, HALO), :] = jnp.where(right_ok, xr_ref[...], 0.0)
    y = jnp.zeros((TM, 3 * 256), F32)
    for j in range(GD_CONV):
        y = y + xext[pl.ds(HALO - GD_CONV // 2 + j, TM), :] * cw_ref[pl.ds(j, 1), :]
    y = _silu(y)
    qv = y[:, 0:256]
    kv = y[:, 256:512]
    q = qv * lax.rsqrt(_group_sum(qv * qv, gg_ref) + EPS) * (GD_HD ** -0.5)
    k = kv * lax.rsqrt(_group_sum(kv * kv, gg_ref) + EPS)
    v = y[:, 512:768]
    ab = ab_ref[...]
    lane = lax.broadcasted_iota(I32, (1, LANES), 1)
    gval = nea_ref[...] * _softplus(ab + dtb_ref[...])
    g1, g2, g3 = _split3(jnp.where(lane < 2 * GD_HEADS, gval, _sigmoid(ab)))
    src3 = jnp.concatenate([g1, g2, g3], axis=1)

    def mm3(xp, yp):
        xh, xl = _split2(xp)
        yh, yl = _split2(yp)
        lhs = jnp.concatenate([xh, xh, xl], axis=1)
        rhs = jnp.concatenate([_block_diag(yh, bm_ref), _block_diag(yl, bm_ref), _block_diag(yh, bm_ref)], axis=0)
        return _dot(lhs, rhs)

    chains = [(d, c) for d in range(2) for c in range(nchunk)]
    each = lambda f, *lists: [f(*a) for a in zip(*lists)]
    rows = [slice(c * GD_CHUNK, (c + 1) * GD_CHUNK) for _, c in chains]
    gexp = [_dot(src3, eg_ref[d]) for d in range(2)]
    bexp = [_dot(src3, eb_ref[d]) for d in range(2)]
    qc = [q[r] for r in rows]
    kc = [k[r] for r in rows]
    vc = [v[r] for r in rows]
    gch = [gexp[d][r] for (d, _), r in zip(chains, rows)]
    bx = [bexp[d][r] for (d, _), r in zip(chains, rows)]
    mask = lambda m: [pmask_ref[d, m] for d, _ in chains]
    cums = [_dot(lmat_ref[d], jnp.concatenate(_split3(g), axis=0)) for (d, _), g in zip(chains, gch)]
    gcx = [t[0:GD_CHUNK] for t in cums]
    rest = [t[GD_CHUNK:2 * GD_CHUNK] for t in cums]
    tot = [t[2 * GD_CHUNK:3 * GD_CHUNK] for t in cums]
    dif = [_dot(lmat2_ref[d], jnp.concatenate(_split3(g * a), axis=0))
           for (d, _), g, a in zip(chains, gch, mask(2))]
    dec = each(lambda m, t: m * jnp.exp(t), mask(0), dif)
    egc = [jnp.exp(t) for t in gcx]
    kb = [t.astype(BF16) for t in kc]
    kq = each(lambda a, b: _dot_nt(jnp.concatenate([a, b.astype(BF16)], axis=0), _block_diag(a, bm_ref)), kb, qc)
    npk = each(lambda b, t, dd, m: b * t[0:GD_CHUNK] * dd * m, bx, kq, dec, mask(1))
    qk = each(lambda t, dd: t[GD_CHUNK:] * dd, kq, dec)
    n8 = [t * imask_ref[0] for t in npk]
    n8sq = each(mm3, n8, n8)
    pinv = each(lambda e, t: e - t, mask(3), n8)
    pinv = each(lambda p, m: p + mm3(p, m), pinv, n8sq)
    n8q4 = each(mm3, n8sq, n8sq)
    pinv = each(lambda p, m: p + mm3(p, m), pinv, n8q4)
    for lvl in range(1, 4):
        pc = each(lambda p, t: mm3(p, t * imask_ref[lvl]), pinv, npk)
        pinv = each(lambda p, t: p - mm3(t, p), pinv, pc)
    u = each(lambda p, a, b: mm3(p, a * b), pinv, vc, bx)
    w = each(lambda p, a, b, e: mm3(p, a * b * e), pinv, kc, bx, egc)
    for i, (d, c) in enumerate(chains):
        u_o[d, rows[i], :] = u[i]
        w_o[d, rows[i], :] = w[i].astype(BF16)
        qk_o[d, rows[i], :] = qk[i].astype(BF16)
        qd_o[d, rows[i], :] = (qc[i] * egc[i]).astype(BF16)
        kd_o[d, rows[i], :] = (kc[i] * jnp.exp(rest[i])).astype(BF16)
        eg_o[d, c * SUBLANES:(c + 1) * SUBLANES, :] = jnp.exp(tot[i][0:SUBLANES])


def _gdn_local(gdqkv, gdab, consts, nl_tiles):
    b, ntok, _ = gdqkv.shape
    nt = ntok // TM
    (cw, nea, dtb, eg, eb, lmat, lmat2, pmask, imask, gg64, bm256) = consts
    hpt = TM // HALO
    nhalo = ntok // HALO
    full = lambda a: pl.BlockSpec(a.shape, lambda i, t: (0,) * a.ndim)
    out = lambda rows: pl.BlockSpec((2, None, rows, 256), lambda i, t: (0, i, t, 0))
    kern = functools.partial(_gdn_local_kernel, nl_tiles=nl_tiles, nt_tiles=nt)
    nchunk = TM // GD_CHUNK
    return pl.pallas_call(
        kern,
        grid=(b, nt),
        in_specs=[pl.BlockSpec((None, TM, 768), lambda i, t: (i, t, 0)),
                  pl.BlockSpec((None, HALO, 768), lambda i, t: (i, jnp.maximum(t * hpt - 1, 0), 0)),
                  pl.BlockSpec((None, HALO, 768), lambda i, t: (i, jnp.minimum((t + 1) * hpt, nhalo - 1), 0)),
                  pl.BlockSpec((None, TM, LANES), lambda i, t: (i, t, 0)),
                  full(cw), full(nea), full(dtb), full(eg), full(eb), full(lmat), full(lmat2), full(pmask),
                  full(imask), full(gg64), full(bm256)],
        out_specs=[out(TM)] * 5 + [out(nchunk * SUBLANES)],
        out_shape=[jax.ShapeDtypeStruct((2, b, ntok, 256), F32)] +
                  [jax.ShapeDtypeStruct((2, b, ntok, 256), BF16)] * 4 +
                  [jax.ShapeDtypeStruct((2, b, nt * nchunk * SUBLANES, 256), F32)],
        scratch_shapes=[pltpu.VMEM((TM + 2 * HALO, 768), F32)],
        compiler_params=_params(("parallel", "parallel")),
        name="gdn_local",
    )(gdqkv, gdqkv, gdqkv, gdab, cw, nea, dtb, eg, eb, lmat, lmat2, pmask, imask, gg64, bm256)


def _gdn_scan_kernel(u_ref, w_ref, qk_ref, qd_ref, kd_ref, eg_ref, bm_ref, o_ref, s_ref):
    d = pl.program_id(1)
    j = pl.program_id(2)
    nchunk = TM // GD_CHUNK

    @pl.when(j == 0)
    def _init():
        s_ref[...] = jnp.zeros_like(s_ref)

    lane = lax.broadcasted_iota(I32, (1, 256), 1)

    def chunk(i, carry):
        c = jnp.where(d == 0, i, nchunk - 1 - i)
        rows = pl.ds(pl.multiple_of(c * GD_CHUNK, GD_CHUNK), GD_CHUNK)
        s = s_ref[...]
        bds = _block_diag(s.astype(BF16), bm_ref)
        ws_qs = _dot(jnp.concatenate([w_ref[rows, :], qd_ref[rows, :]], axis=0), bds)
        vnew = u_ref[rows, :] - ws_qs[0:GD_CHUNK]
        vb = vnew.astype(BF16)
        o_ref[rows, :] = ws_qs[GD_CHUNK:] + _dot(qk_ref[rows, :], _block_diag(vb, bm_ref))
        z = _dot_tn(kd_ref[rows, :], vb)
        eg = eg_ref[pl.ds(pl.multiple_of(c * SUBLANES, SUBLANES), 1), :]
        snew = s * eg
        for h in range(GD_HEADS):
            snew = snew + jnp.where(lane // GD_HD == h, z[h * GD_HD:(h + 1) * GD_HD], 0.0)
        s_ref[...] = snew
        return carry

    lax.fori_loop(0, nchunk, chunk, 0)


def _gdn_scan(u, w, qk, qd, kd, eg, bm256, nl_tiles):
    _, b, ntok, _ = u.shape
    nt = ntok // TM
    nc_tiles = nt - nl_tiles
    nchunk = TM // GD_CHUNK

    def tile_index(d, j):
        is_ctx = j < nc_tiles
        fwd = jnp.where(is_ctx, nl_tiles + j, j - nc_tiles)
        bwd = jnp.where(is_ctx, nl_tiles + nc_tiles - 1 - j, nl_tiles - 1 - (j - nc_tiles))
        return jnp.where(d == 0, fwd, bwd)

    tile = lambda: pl.BlockSpec((None, None, TM, 256), lambda i, d, j: (d, i, tile_index(d, j), 0))
    return pl.pallas_call(
        _gdn_scan_kernel,
        grid=(b, 2, nt),
        in_specs=[tile(), tile(), tile(), tile(), tile(),
                  pl.BlockSpec((None, None, nchunk * SUBLANES, 256), lambda i, d, j: (d, i, tile_index(d, j), 0)),
                  pl.BlockSpec(bm256.shape, lambda i, d, j: (0, 0))],
        out_specs=tile(),
        out_shape=jax.ShapeDtypeStruct((2, b, ntok, 256), F32),
        scratch_shapes=[pltpu.VMEM((GD_HD, 256), F32)],
        compiler_params=_params(("parallel", "arbitrary", "arbitrary")),
        name="gdn_scan",
    )(u, w, qk, qd, kd, eg, bm256)


def _merge_kernel(x_ref, mod_ref, g1_ref, g2_ref, wg_ref, bg_ref, oa_ref, ob_ref, om_ref, of_ref, ob2_ref, gate_ref,
                  gdog_ref, gg_ref, wbr_ref, wo_ref, wrh_ref, wrl_ref, br_ref,
                  xn_o, h2_o, idx_o, gates_o):
    x = x_ref[...]
    mod = mod_ref[...]
    dm = D_MODEL
    sh1, sc1, gt1 = mod[:, 0:dm], mod[:, dm:2 * dm], mod[:, 2 * dm:3 * dm]
    sh2, sc2 = mod[:, 3 * dm:4 * dm], mod[:, 4 * dm:5 * dm]
    hb = (_rms_rows(x, g1_ref[...]) * (1.0 + sc1) + sh1).astype(BF16)

    o = of_ref[...] + ob2_ref[...]
    ss = _group_sum(o * o, gg_ref)
    od = (o * lax.rsqrt(ss * (1.0 / GD_HD) + EPS) * gdog_ref[...] * _silu(gate_ref[...])).astype(BF16)

    acc = jnp.zeros((TM, dm), F32)
    for n, br in enumerate((oa_ref[...], ob_ref[...], om_ref[...], od)):
        gpre = _dot(hb, wg_ref[:, n * dm:(n + 1) * dm]) + bg_ref[:, n * dm:(n + 1) * dm]
        acc = acc + _sigmoid(gpre) * _dot(br, wbr_ref[n])
    y = _dot(acc.astype(BF16), wo_ref[...])
    xn = x + gt1 * y
    xn_o[...] = xn
    h2 = _rms_rows(xn, g2_ref[...]) * (1.0 + sc2) + sh2
    h2_o[...] = h2

    hh, hl = _split2(h2)
    logits = _dot(hh, wrh_ref[...]) + _dot(hl, wrh_ref[...]) + _dot(hh, wrl_ref[...]) + br_ref[...]
    lane = lax.broadcasted_iota(I32, logits.shape, 1)
    vals = jnp.full(logits.shape, NEG_BIG, F32)
    idxs = jnp.zeros(logits.shape, I32)
    l = logits
    for k in range(TOP_K):
        m = jnp.max(l, axis=-1, keepdims=True)
        i = jnp.min(jnp.where(l == m, lane, LANES), axis=-1, keepdims=True)
        vals = jnp.where(lane == k, m, vals)
        idxs = jnp.where(lane == k, i, idxs)
        l = jnp.where(lane == i, NEG_BIG * 2.0, l)
    e = jnp.exp(vals - jnp.max(vals, axis=-1, keepdims=True))
    e = jnp.where(lane < TOP_K, e, 0.0)
    gates_o[...] = e / jnp.sum(e, axis=-1, keepdims=True)
    idx_o[...] = idxs


def _merge(x_all, modsel, g1, g2, wg, bg, oa, ob, om, o2, gate, gdog, gg64, wbr, wo, wrh, wrl, brp, n_tiles, nl_tiles):
    b = x_all.shape[0]
    ntok_out = n_tiles * TM
    tok = lambda w: pl.BlockSpec((None, TM, w), lambda i, t: (i, t, 0))
    full = lambda a: pl.BlockSpec(a.shape, lambda i, t: (0,) * a.ndim)
    odir = lambda dd: pl.BlockSpec((None, None, TM, 256), lambda i, t: (dd, i, t, 0))
    return pl.pallas_call(
        _merge_kernel,
        grid=(b, n_tiles),
        in_specs=[tok(D_MODEL),
                  pl.BlockSpec((None, None, 1, 6 * D_MODEL), lambda i, t: (i, t // nl_tiles, 0, 0)),
                  full(g1), full(g2), full(wg), full(bg), tok(256), tok(256), tok(256), odir(0), odir(1), tok(256),
                  full(gdog), full(gg64), full(wbr), full(wo), full(wrh), full(wrl), full(brp)],
        out_specs=[tok(D_MODEL), tok(D_MODEL), tok(LANES), tok(LANES)],
        out_shape=[jax.ShapeDtypeStruct((b, ntok_out, D_MODEL), F32), jax.ShapeDtypeStruct((b, ntok_out, D_MODEL), F32),
                   jax.ShapeDtypeStruct((b, ntok_out, LANES), I32), jax.ShapeDtypeStruct((b, ntok_out, LANES), F32)],
        compiler_params=_params(("parallel", "parallel")),
        name="merge",
    )(x_all, modsel, g1, g2, wg, bg, oa, ob, om, o2, o2, gate, gdog, gg64, wbr, wo, wrh, wrl, brp)


def _expert_kernel(bexp_ref, src0_ref, src_ref, dst_ref, tok_hbm, wup_ref, bup_ref, wdn_ref, bdn_ref, out_hbm,
                   x0, x1, x2, x3, y0, y1, y2, y3, gsem, ssem):
    s = pl.program_id(0)
    ns = pl.num_programs(0)
    xs = (x0, x1, x2, x3)
    ys = (y0, y1, y2, y3)

    def issue_gather(idx_ref, blk, k):
        for r in range(EXP_SUB):
            pltpu.make_async_copy(tok_hbm.at[pl.ds(idx_ref[blk, r], 1), :], xs[k].at[pl.ds(r, 1), :], gsem.at[k]).start()

    def wait_gather(k):
        pltpu.make_async_copy(tok_hbm.at[pl.ds(0, EXP_SUB), :], xs[k], gsem.at[k]).wait()

    def issue_scatter(k):
        for r in range(EXP_SUB):
            pltpu.make_async_copy(ys[k].at[pl.ds(r, 1), :], out_hbm.at[pl.ds(dst_ref[k, r], 1), :], ssem.at[k]).start()

    def wait_scatter(k):
        pltpu.make_async_copy(ys[k], out_hbm.at[pl.ds(0, EXP_SUB), :], ssem.at[k]).wait()

    @pl.when(s == 0)
    def _prologue():
        issue_gather(src0_ref, 0, 0)
        issue_gather(src0_ref, 1, 1)

    for k in range(EXP_NSUB):
        wait_gather(k)

        @pl.when(s > 0)
        def _free_y():
            wait_scatter(k)

        if k >= 1:
            issue_scatter(k - 1)
        issue_gather(src_ref, k, (k + 2) % EXP_NSUB)
        xb = xs[k][...].astype(BF16)
        z = _dot(xb, wup_ref[...]) + bup_ref[...]
        glu = jnp.minimum(z[:, :D_FF], SWIGLU_LIMIT)
        lin = jnp.clip(z[:, D_FF:], -SWIGLU_LIMIT, SWIGLU_LIMIT)
        act = glu * _sigmoid(SWIGLU_ALPHA * glu) * (lin + 1.0)
        ys[k][...] = _dot(act.astype(BF16), wdn_ref[...]) + bdn_ref[...]
    issue_scatter(EXP_NSUB - 1)

    @pl.when(s == ns - 1)
    def _epilogue():
        for k in range(EXP_NSUB):
            wait_scatter(k)
        wait_gather(0)
        wait_gather(1)


def _experts(blk_exp, row_src, row_dst, tok, wup, bup, wdn, bdn):
    ns = blk_exp.shape[0]
    n_rows = ns * EXP_TILE
    src = jnp.concatenate([row_src, jnp.zeros((2 * EXP_SUB,), I32)])
    src0 = src[:2 * EXP_SUB].reshape(2, EXP_SUB)
    src_ahead = jnp.concatenate([src[2 * EXP_SUB:], jnp.zeros((2 * EXP_SUB,), I32)])[:n_rows]
    idx = lambda: pl.BlockSpec((None, EXP_NSUB, EXP_SUB), lambda i, be: (i, 0, 0), memory_space=pltpu.SMEM)
    gs = pltpu.PrefetchScalarGridSpec(
        num_scalar_prefetch=1,
        grid=(ns,),
        in_specs=[pl.BlockSpec(memory_space=pltpu.SMEM),
                  idx(), idx(),
                  pl.BlockSpec(memory_space=pl.ANY),
                  pl.BlockSpec((None, D_MODEL, 2 * D_FF), lambda i, be: (be[i], 0, 0)),
                  pl.BlockSpec((None, 1, 2 * D_FF), lambda i, be: (be[i], 0, 0)),
                  pl.BlockSpec((None, D_FF, D_MODEL), lambda i, be: (be[i], 0, 0)),
                  pl.BlockSpec((None, 1, D_MODEL), lambda i, be: (be[i], 0, 0))],
        out_specs=pl.BlockSpec(memory_space=pl.ANY),
        scratch_shapes=[pltpu.VMEM((EXP_SUB, D_MODEL), F32)] * (2 * EXP_NSUB) +
                       [pltpu.SemaphoreType.DMA((EXP_NSUB,)), pltpu.SemaphoreType.DMA((EXP_NSUB,))],
    )
    return pl.pallas_call(
        _expert_kernel,
        grid_spec=gs,
        out_shape=jax.ShapeDtypeStruct((n_rows, D_MODEL), F32),
        compiler_params=_params(("arbitrary",)),
        name="experts",
    )(blk_exp, src0, src_ahead.reshape(ns, EXP_NSUB, EXP_SUB), row_dst.reshape(ns, EXP_NSUB, EXP_SUB),
      tok, wup, bup, wdn, bdn)


def _combine_kernel(x_ref, mod_ref, gates_ref, y0_ref, y1_ref, y2_ref, y3_ref, o_ref):
    g2 = mod_ref[...][:, 5 * D_MODEL:6 * D_MODEL]
    gates = gates_ref[...]
    f = jnp.zeros((TM, D_MODEL), F32)
    for k, y_ref in enumerate((y0_ref, y1_ref, y2_ref, y3_ref)):
        f = f + gates[:, k:k + 1] * y_ref[...]
    o_ref[...] = x_ref[...] + g2 * f


def _combine(xn, modsel, gates, out4, nl_tiles):
    b, ntok, _ = xn.shape
    nt = ntok // TM
    slab = b * nt
    tok = lambda w: pl.BlockSpec((None, TM, w), lambda i, t: (i, t, 0))
    ys = lambda k: pl.BlockSpec((TM, D_MODEL), lambda i, t: (k * slab + i * nt + t, 0))
    return pl.pallas_call(
        _combine_kernel,
        grid=(b, nt),
        in_specs=[tok(D_MODEL),
                  pl.BlockSpec((None, None, 1, 6 * D_MODEL), lambda i, t: (i, t // nl_tiles, 0, 0)),
                  tok(LANES), ys(0), ys(1), ys(2), ys(3)],
        out_specs=tok(D_MODEL),
        out_shape=jax.ShapeDtypeStruct((b, ntok, D_MODEL), F32),
        compiler_params=_params(("parallel", "parallel")),
        name="combine",
    )(xn, modsel, gates, out4, out4, out4, out4)


def _np_consts():
    lane = np.arange(256)
    half, r = lane // 128, lane % 128
    s, j = r // 16, r % 16
    d = np.where(j < 8, j + 8 * half, 16 + (j - 8) + 8 * half)
    perm_da = s * 32 + d
    dperm_da = d
    h, j = r // 32, r % 32
    d = np.where(j < 16, j + 16 * half, 32 + (j - 16) + 16 * half)
    perm_gq = h * 64 + d
    perm_gk = (h // 2) * 64 + d
    dperm_ga = d
    hv, dv = lane // 64, lane % 64
    perm_gv = (hv // 2) * 64 + dv
    grp = lambda key: (key[:, None] == key[None, :]).astype(np.float32)
    gda = grp((lane % 128) // 16)
    gga = grp((lane % 128) // 32)
    g64 = grp(lane // 64)
    stack2 = lambda g: np.concatenate([g, g], axis=0)
    bm256 = (np.arange(256)[:, None] // 64 == lane[None, :] // 64).astype(np.float32)
    bm512 = (np.arange(512)[:, None] // 128 == lane[None, :] // 64).astype(np.float32)
    i = np.arange(64)
    jj = lane % 64
    lmat, lmat2, pmask = [], [], []
    for dirn in range(2):
        if dirn == 0:
            inc = (i[None, :] <= i[:, None]); aft = (i[None, :] > i[:, None])
            p_incl = (i[:, None] >= jj[None, :]); p_strict = (i[:, None] > jj[None, :]); p_after = (i[:, None] > jj[None, :])
        else:
            inc = (i[None, :] >= i[:, None]); aft = (i[None, :] < i[:, None])
            p_incl = (i[:, None] <= jj[None, :]); p_strict = (i[:, None] < jj[None, :]); p_after = (i[:, None] < jj[None, :])
        ones = np.ones((64, 64))
        l1 = np.concatenate([inc, aft, ones], axis=0).astype(np.float32)
        lmat.append(np.concatenate([l1, l1, l1], axis=1))
        l2 = inc.astype(np.float32)
        lmat2.append(np.concatenate([l2, l2, l2], axis=1))
        eye = (i[:, None] == jj[None, :])
        pmask.append(np.stack([p_incl, p_strict, p_after, eye]).astype(np.float32))
    bi, bj = i[:, None], jj[None, :]
    imask = np.stack([bi // 8 == bj // 8,
                      (bi // 16 == bj // 16) & (bi // 8 != bj // 8),
                      (bi // 32 == bj // 32) & (bi // 16 != bj // 16),
                      bi // 32 != bj // 32]).astype(np.float32)
    eg = np.zeros((2, 128, 256), np.float32)
    eb = np.zeros((2, 128, 256), np.float32)
    for dirn in range(2):
        for hh in range(GD_HEADS):
            eg[dirn, dirn * GD_HEADS + hh, hh * 64:(hh + 1) * 64] = 1.0
            eb[dirn, 2 * GD_HEADS + dirn * GD_HEADS + hh, hh * 64:(hh + 1) * 64] = 1.0
    eg = np.concatenate([eg, eg, eg], axis=1)
    eb = np.concatenate([eb, eb, eb], axis=1)
    return dict(perm_da=perm_da, dperm_da=dperm_da, perm_gq=perm_gq, perm_gk=perm_gk, dperm_ga=dperm_ga, perm_gv=perm_gv,
                gda=stack2(gda), gga=stack2(gga), g64=stack2(g64), bm256=bm256, bm512=bm512,
                lmat=np.stack(lmat), lmat2=np.stack(lmat2), pmask=np.stack(pmask), imask=imask, eg=eg, eb=eb)


def _rope_tables(seq, ctx_len, quarter, reps):
    t = jnp.arange(seq)
    row = (t // GRID_W).astype(F32)
    col = (t % GRID_W).astype(F32)
    inv = ROPE_THETA ** (-jnp.arange(quarter, dtype=F32) / quarter)
    ang = jnp.concatenate([row[:, None] * inv, col[:, None] * inv], axis=1)
    ang = jnp.tile(ang, (1, reps))
    cos = jnp.concatenate([jnp.cos(ang), jnp.ones((ctx_len, LANES), F32)], axis=0)
    sin = jnp.concatenate([jnp.sin(ang), jnp.zeros((ctx_len, LANES), F32)], axis=0)
    return cos, sin


def _routing(idx, n_tok):
    n_asg = n_tok * TOP_K
    e_flat = idx.reshape(-1)
    order = jnp.argsort(e_flat, stable=True).astype(I32)
    counts = jnp.sum((e_flat[:, None] == jnp.arange(N_EXPERTS, dtype=I32)[None, :]).astype(I32), axis=0)
    padded = (counts + EXP_TILE - 1) // EXP_TILE * EXP_TILE
    start = jnp.cumsum(counts) - counts
    pend = jnp.cumsum(padded)
    pstart = pend - padded
    n_tiles = (n_asg + N_EXPERTS * (EXP_TILE - 1)) // EXP_TILE + 1
    n_rows = n_tiles * EXP_TILE
    tile_row0 = jnp.arange(n_tiles, dtype=I32) * EXP_TILE
    tile_exp = jnp.minimum(jnp.sum((pend[None, :] <= tile_row0[:, None]).astype(I32), axis=1), N_EXPERTS - 1)
    row = jnp.arange(n_rows, dtype=I32)
    e_row = jnp.repeat(tile_exp, EXP_TILE)
    local = row - pstart[e_row]
    valid = local < counts[e_row]
    asg = order[jnp.minimum(start[e_row] + local, n_asg - 1)]
    tok_of = asg // TOP_K
    row_src = jnp.where(valid, tok_of, 0)
    n_pad_before = row - (start[e_row] + counts[e_row])
    row_dst = jnp.where(valid, (asg % TOP_K) * n_tok + tok_of, n_asg + n_pad_before)
    return tile_exp.astype(I32), row_src.astype(I32), row_dst.astype(I32)


def kernel(x, c, ctx, c_ctx, w_mod, b_mod, norm1_g, norm2_g, w_in, b_gate, da_q_g, da_k_g, da_lam_q1, da_lam_k1,
           da_lam_q2, da_lam_k2, da_sub_g, ga_q_g, ga_k_g, gm_v_g, gm_ws, gm_bs, gd_conv_w, gd_a_log, gd_dt_bias,
           gd_out_g, w_br, w_o, w_router, b_router, w_up, b_up, w_down, b_down):
    bsz, seq, dm = x.shape
    ctx_len = ctx.shape[1]
    depth = w_mod.shape[0]
    assert dm == D_MODEL and seq % TM == 0 and ctx_len % TM == 0 and seq % ctx_len == 0 and seq % GRID_W == 0
    ntok = seq + ctx_len
    nl_tiles = seq // TM
    nt_tiles = ntok // TM
    cn = _np_consts()
    bfc = lambda a: jnp.asarray(a, BF16)
    gda, gga, g64 = bfc(cn['gda']), bfc(cn['gga']), bfc(cn['g64'])
    bm256, bm512 = bfc(cn['bm256']), bfc(cn['bm512'])
    lmat, lmat2 = bfc(cn['lmat']), bfc(cn['lmat2'])
    pmask = jnp.asarray(cn['pmask'], F32)
    imask = jnp.asarray(cn['imask'], F32)
    eg, eb = bfc(cn['eg']), bfc(cn['eb'])
    cosa, sina = _rope_tables(seq, ctx_len, DA_HD // 4, LANES // (DA_HD // 2))
    cosg, sing = _rope_tables(seq, ctx_len, GA_HD // 4, LANES // (GA_HD // 2))
    cuts = np.cumsum((256, 256, 256, 256, 128, 128, 256, 256, 768, 256, 8, 8))
    mrows = -(-(bsz + 1) // SUBLANES) * SUBLANES
    cpad = jnp.zeros((mrows, dm), F32).at[:bsz].set(c).at[bsz].set(c_ctx)

    x_all = jnp.concatenate([x, ctx], axis=1)
    for l in range(depth):
        need_ctx = l < depth - 1
        lam_init = 0.8 - 0.6 * math.exp(-0.3 * l)
        mod = _adaln(cpad, w_mod[l], b_mod[l])
        modsel = jnp.stack([mod[:bsz], jnp.broadcast_to(mod[bsz], (bsz, 6 * dm))], axis=1).reshape(bsz, 2, 1, 6 * dm)

        wl = w_in[l]
        aq, ak, av, gq, gk, gv, mu, mv, dqkv, dgate, dda, ddb = jnp.split(wl[:, :cuts[-1]], cuts[:-1].tolist(), axis=1)
        wcat = jnp.concatenate([aq[:, cn['perm_da']], ak[:, cn['perm_da']], av, gq[:, cn['perm_gq']],
                                gk[:, cn['perm_gk']], gv[:, cn['perm_gv']], mu, mv, dqkv, dgate, dda, ddb,
                                jnp.zeros((dm, LANES - 4 * GD_HEADS), F32)], axis=1).astype(BF16)
        row = lambda a: a.reshape(1, -1).astype(F32)
        daqg = row(da_q_g[l][cn['dperm_da']])
        dakg = row(da_k_g[l][cn['dperm_da']])
        gaqg = row(ga_q_g[l][cn['dperm_ga']])
        gakg = row(ga_k_g[l][cn['dperm_ga']])
        wsp = gm_ws[l].transpose(1, 0, 2).reshape(GM_CHUNK, GM_GROUPS * GM_CHUNK).astype(BF16)
        bsp = jnp.repeat(gm_bs[l].T, GM_GW, axis=1)
        consts = (gda, gga, cosa, sina, cosg, sing, daqg, dakg, gaqg, gakg, row(gm_v_g[l]), wsp, bsp, bm512)
        daq, dak, dav, gaq, gak, gav, om, gdqkv, gdgate, gdab = _inproj(
            x_all, modsel, row(norm1_g[l]), wcat, consts, nl_tiles)

        lam = (jnp.exp(jnp.sum(da_lam_q1[l] * da_lam_k1[l])) - jnp.exp(jnp.sum(da_lam_q2[l] * da_lam_k2[l]))
               + lam_init).reshape(1).astype(F32)
        subg = row(jnp.tile(da_sub_g[l], DA_HEADS))
        att = functools.partial(_attention, lam, gg64=g64, subg=subg)
        lat = dict(q_start=0, q_len=seq, kv_start=0, kv_len=ntok)
        oa = att(daq, dak, dav, diff=True, post_scale=1.0 - lam_init, **lat)
        ob = att(gaq, gak, gav, diff=False, post_scale=1.0, **lat)
        if need_ctx:
            cx = dict(q_start=seq, q_len=ctx_len, kv_start=seq, kv_len=ctx_len)
            oa = jnp.concatenate([oa, att(daq, dak, dav, diff=True, post_scale=1.0 - lam_init, **cx)], axis=1)
            ob = jnp.concatenate([ob, att(gaq, gak, gav, diff=False, post_scale=1.0, **cx)], axis=1)

        nea = jnp.zeros((1, LANES), F32).at[0, :2 * GD_HEADS].set(-jnp.exp(gd_a_log[l].reshape(-1)))
        dtb = jnp.zeros((1, LANES), F32).at[0, :2 * GD_HEADS].set(gd_dt_bias[l].reshape(-1))
        gconsts = (gd_conv_w[l], nea, dtb, eg, eb, lmat, lmat2, pmask, imask, g64, bm256)
        u, w, qk, qd, kd, egl = _gdn_local(gdqkv, gdab, gconsts, nl_tiles)
        o2 = _gdn_scan(u, w, qk, qd, kd, egl, bm256, nl_tiles)

        n_tiles = nt_tiles if need_ctx else nl_tiles
        wg = wl[:, cuts[-1]:].astype(BF16)
        wr = jnp.zeros((dm, LANES), F32).at[:, :N_EXPERTS].set(w_router[l])
        wrh = wr.astype(BF16)
        wrl = (wr - wrh.astype(F32)).astype(BF16)
        brp = jnp.full((1, LANES), NEG_BIG, F32).at[0, :N_EXPERTS].set(b_router[l])
        xn, h2, idx, gates = _merge(x_all, modsel, row(norm1_g[l]), row(norm2_g[l]), wg, row(b_gate[l]),
                                    oa, ob, om, o2, gdgate, row(jnp.tile(gd_out_g[l], GD_HEADS)), g64,
                                    w_br[l].astype(BF16), w_o[l].astype(BF16), wrh, wrl, brp, n_tiles, nl_tiles)

        n_tok = bsz * n_tiles * TM
        tile_exp, row_src, row_dst = _routing(idx[:, :, :TOP_K], n_tok)
        out4 = _experts(tile_exp, row_src, row_dst, h2.reshape(n_tok, dm), w_up[l].astype(BF16),
                        b_up[l].reshape(N_EXPERTS, 1, 2 * D_FF), w_down[l].astype(BF16),
                        b_down[l].reshape(N_EXPERTS, 1, dm))
        x_all = _combine(xn, modsel, gates, out4, nl_tiles)
    return x_all[:, :seq] if x_all.shape[1] != seq else x_all
```

```python
import functools
import math

import numpy as np
import jax
import jax.numpy as jnp
from jax import lax
from jax.experimental import pallas as pl
from jax.experimental.pallas import tpu as pltpu

F32 = jnp.float32
BF16 = jnp.bfloat16
I32 = jnp.int32

D_MODEL = 1024
GRID_W = 64
EPS = 1e-6
ROPE_THETA = 10000.0
DA_HEADS, DA_HD = 4, 32
GA_HEADS, GA_KV, GA_HD = 4, 2, 64
GM_GROUPS, GM_GW, GM_CHUNK = 4, 64, 128
GD_HEADS, GD_HD, GD_CONV, GD_CHUNK = 4, 64, 5, 64
N_BRANCH, BR_W = 4, 256
N_EXPERTS, TOP_K, D_FF = 32, 4, 1024
SWIGLU_ALPHA, SWIGLU_LIMIT = 1.702, 7.0
EXP_SUB = 256
EXP_NSUB = 4
EXP_TILE = EXP_SUB * EXP_NSUB
DMA_PRIORITIES = 2

LANES = 128
SUBLANES = 8
TM = 256
TQ = 128
HALO = SUBLANES
ROW_TILES = D_MODEL // LANES
VMEM_LIMIT = 56 * 1024 * 1024
NEG_BIG = -1e30
LOG2E = math.log2(math.e)

C_DAQ, C_DAK, C_DAV, C_GAQ, C_GAK, C_GAV, C_GMU, C_GMV, C_GDQKV, C_GDGATE, C_GDAB, C_END = (
    0, 256, 512, 768, 1024, 1280, 1536, 1792, 2048, 2816, 3072, 3200)


def _params(sem):
    return pltpu.CompilerParams(dimension_semantics=sem, vmem_limit_bytes=VMEM_LIMIT)


def _split2(x):
    hi = x.astype(BF16)
    lo = (x - hi.astype(F32)).astype(BF16)
    return hi, lo


def _split3(x):
    hi = x.astype(BF16)
    r = x - hi.astype(F32)
    mid = r.astype(BF16)
    lo = (r - mid.astype(F32)).astype(BF16)
    return hi, mid, lo


def _dot(a, b):
    return jnp.dot(a, b, preferred_element_type=F32)


def _dot_nt(a, b):
    return lax.dot_general(a, b, (((1,), (1,)), ((), ())), preferred_element_type=F32)


def _dot_tn(a, b):
    return lax.dot_general(a, b, (((0,), (0,)), ((), ())), preferred_element_type=F32)


def _group_sum(x, gg_ref):
    hi, lo = _split2(x)
    return _dot(jnp.concatenate([hi, lo], axis=1), gg_ref[...])


def _sigmoid(x):
    return 1.0 / (1.0 + jnp.exp(-x))


def _silu(x):
    return x * _sigmoid(x)


def _gelu(x):
    return 0.5 * x * (1.0 + lax.erf(x * (1.0 / math.sqrt(2.0))))


def _softplus(x):
    return jnp.maximum(x, 0.0) + jnp.log1p(jnp.exp(-jnp.abs(x)))


def _rms_rows(x, gain):
    return x * lax.rsqrt(jnp.mean(x * x, axis=-1, keepdims=True) + EPS) * gain


def _block_diag(y, mask_ref):
    return jnp.concatenate([y, y, y, y], axis=0) * mask_ref[...]


def _adaln_kernel(c_ref, w_ref, b_ref, o_ref):
    a = _silu(c_ref[...])
    ah, al = _split2(a)
    w = w_ref[...]
    wh, wl = _split2(w)
    o_ref[...] = _dot(ah, wh) + _dot(al, wh) + _dot(ah, wl) + b_ref[...]


def _adaln(cpad, w_mod, b_mod):
    m = cpad.shape[0]
    n = w_mod.shape[1]
    tn = 512
    return pl.pallas_call(
        _adaln_kernel,
        grid=(n // tn,),
        in_specs=[pl.BlockSpec((m, D_MODEL), lambda j: (0, 0)),
                  pl.BlockSpec((D_MODEL, tn), lambda j: (0, j)),
                  pl.BlockSpec((1, tn), lambda j: (0, j))],
        out_specs=pl.BlockSpec((m, tn), lambda j: (0, j)),
        out_shape=jax.ShapeDtypeStruct((m, n), F32),
        compiler_params=_params(("arbitrary",)),
        name="adaln",
    )(cpad, w_mod, b_mod.reshape(1, n))


def _qk_prep(z, gg_ref, nd, gain_ref, cos_ref, sin_ref, scale):
    ss = _group_sum(z * z, gg_ref)
    y = z * lax.rsqrt(ss * (1.0 / nd) + EPS) * gain_ref[...]
    a = y[:, :LANES]
    b = y[:, LANES:]
    c = cos_ref[...]
    s = sin_ref[...]
    out = jnp.concatenate([a * c - b * s, b * c + a * s], axis=1)
    if scale != 1.0:
        out = out * scale
    return out


def _inproj_kernel(x_ref, mod_ref, g1_ref, w_ref, gda_ref, gga_ref, cosa_ref, sina_ref, cosg_ref, sing_ref,
                   daqg_ref, dakg_ref, gaqg_ref, gakg_ref, gmvg_ref, wsp_ref, bsp_ref, bm_ref,
                   daq_o, dak_o, dav_o, gaq_o, gak_o, gav_o, om_o, gdqkv_o, gdgate_o, gdab_o):
    x = x_ref[...]
    mod = mod_ref[...]
    sh = mod[:, 0:D_MODEL]
    sc = mod[:, D_MODEL:2 * D_MODEL]
    hb = (_rms_rows(x, g1_ref[...]) * (1.0 + sc) + sh).astype(BF16)

    def proj(lo, hi):
        return _dot(hb, w_ref[:, lo:hi])

    daq_o[...] = _qk_prep(proj(C_DAQ, C_DAK), gda_ref, DA_HD, daqg_ref, cosa_ref, sina_ref, LOG2E * DA_HD ** -0.5).astype(BF16)
    dak_o[...] = _qk_prep(proj(C_DAK, C_DAV), gda_ref, DA_HD, dakg_ref, cosa_ref, sina_ref, 1.0).astype(BF16)
    dav_o[...] = proj(C_DAV, C_GAQ).astype(BF16)
    gaq_o[...] = _qk_prep(proj(C_GAQ, C_GAK), gga_ref, GA_HD, gaqg_ref, cosg_ref, sing_ref, LOG2E * GA_HD ** -0.5).astype(BF16)
    gak_o[...] = _qk_prep(proj(C_GAK, C_GAV), gga_ref, GA_HD, gakg_ref, cosg_ref, sing_ref, 1.0).astype(BF16)
    gav_o[...] = proj(C_GAV, C_GMU).astype(BF16)

    u = _gelu(proj(C_GMU, C_GMV))
    v = _rms_rows(_gelu(proj(C_GMV, C_GDQKV)), gmvg_ref[...])
    for j in range(TM // GM_CHUNK):
        rows = slice(j * GM_CHUNK, (j + 1) * GM_CHUNK)
        bd = _block_diag(v[rows].astype(BF16), bm_ref)
        mixed = _dot(wsp_ref[...], bd) + bsp_ref[...]
        om_o[rows, :] = (u[rows] * mixed).astype(BF16)

    gdqkv_o[...] = proj(C_GDQKV, C_GDGATE)
    gdgate_o[...] = proj(C_GDGATE, C_GDAB)
    gdab_o[...] = proj(C_GDAB, C_END)


def _inproj(x_all, modsel, g1, wcat, consts, nl_tiles):
    b, ntok, _ = x_all.shape
    nt = ntok // TM
    tok = lambda w: pl.BlockSpec((None, TM, w), lambda i, t: (i, t, 0))
    full = lambda a: pl.BlockSpec(a.shape, lambda i, t: (0,) * a.ndim)
    tab = pl.BlockSpec((TM, LANES), lambda i, t: (t, 0))
    (gda, gga, cosa, sina, cosg, sing, daqg, dakg, gaqg, gakg, gmvg, wsp, bsp, bm512) = consts
    in_specs = [tok(D_MODEL),
                pl.BlockSpec((None, None, 1, 6 * D_MODEL), lambda i, t: (i, t // nl_tiles, 0, 0)),
                full(g1), full(wcat), full(gda), full(gga), tab, tab, tab, tab,
                full(daqg), full(dakg), full(gaqg), full(gakg), full(gmvg), full(wsp), full(bsp), full(bm512)]
    widths = [(256, BF16)] * 7 + [(768, F32), (256, F32), (128, F32)]
    return pl.pallas_call(
        _inproj_kernel,
        grid=(b, nt),
        in_specs=in_specs,
        out_specs=[tok(w) for w, _ in widths],
        out_shape=[jax.ShapeDtypeStruct((b, ntok, w), dt) for w, dt in widths],
        compiler_params=_params(("parallel", "parallel")),
        name="inproj",
    )(x_all, modsel, g1, wcat, gda, gga, cosa, sina, cosg, sing, daqg, dakg, gaqg, gakg, gmvg, wsp, bsp, bm512)


def _attn_kernel(lam_ref, q_ref, k_ref, v_ref, gg_ref, subg_ref, o_ref, *, diff, qlanes, post_scale):
    q = q_ref[...]
    k = k_ref[...]
    v = v_ref[...]
    tq = q.shape[0]
    lane = lax.broadcasted_iota(I32, (1, 2 * LANES), 1)
    qhead = (lane % LANES) // qlanes
    vhead = lane // 64
    lam = lam_ref[0]
    zero = jnp.zeros_like(q)

    def scores(h):
        lhs = jnp.concatenate([jnp.where(qhead == 2 * h, q, zero), jnp.where(qhead == 2 * h + 1, q, zero)], axis=0)
        return _dot_nt(lhs, k)

    def softmax(s):
        m = jnp.max(s, axis=-1, keepdims=True)
        p = jnp.exp2(s - m)
        l = jnp.sum(p, axis=-1, keepdims=True)
        return p, l

    nloop = DA_HEADS if diff else GA_HEADS // 2
    acc = jnp.zeros((tq, 2 * LANES), F32)
    s_next = scores(0)
    for h in range(nloop):
        s_cur = s_next
        if h + 1 < nloop:
            s_next = scores(h + 1)
        p, l = softmax(s_cur)
        if diff:
            a = (p[:tq] - p[tq:] * (lam * l[:tq] / l[tq:])).astype(BF16)
            acc = acc + jnp.where(vhead == h, _dot(a, v) / l[:tq], 0.0)
        else:
            o = _dot(p.astype(BF16), v) / l
            acc = acc + jnp.where(vhead == 2 * h, o[:tq], 0.0) + jnp.where(vhead == 2 * h + 1, o[tq:], 0.0)
    if diff:
        ss = _group_sum(acc * acc, gg_ref)
        acc = acc * lax.rsqrt(ss * (1.0 / (2 * DA_HD)) + EPS) * subg_ref[...] * post_scale
    o_ref[...] = acc.astype(BF16)


def _attention(lam, q, k, v, gg64, subg, *, diff, post_scale, q_start, q_len, kv_start, kv_len):
    b = q.shape[0]
    qoff = q_start // TQ
    kvoff = kv_start // kv_len
    kern = functools.partial(_attn_kernel, diff=diff, qlanes=16 if diff else 32, post_scale=post_scale)
    full = lambda a: pl.BlockSpec(a.shape, lambda i, j: (0,) * a.ndim)
    kvspec = pl.BlockSpec((None, kv_len, 256), lambda i, j: (i, kvoff, 0))
    return pl.pallas_call(
        kern,
        grid=(b, q_len // TQ),
        in_specs=[pl.BlockSpec(memory_space=pltpu.SMEM),
                  pl.BlockSpec((None, TQ, 256), lambda i, j: (i, qoff + j, 0)),
                  kvspec, kvspec, full(gg64), full(subg)],
        out_specs=pl.BlockSpec((None, TQ, 256), lambda i, j: (i, j, 0)),
        out_shape=jax.ShapeDtypeStruct((b, q_len, 256), BF16),
        compiler_params=_params(("parallel", "parallel")),
        name="diff_attn" if diff else "gqa_attn",
    )(lam, q, k, v, gg64, subg)


def _gdn_local_kernel(xm_ref, xl_ref, xr_ref, ab_ref, cw_ref, nea_ref, dtb_ref, eg_ref, eb_ref,
                      lmat_ref, lmat2_ref, pmask_ref, imask_ref, gg_ref, bm_ref,
                      u_o, w_o, qk_o, qd_o, kd_o, eg_o, xext, *, nl_tiles, nt_tiles):
    t = pl.program_id(1)
    nchunk = TM // GD_CHUNK

    left_ok = jnp.logical_and(t != 0, t != nl_tiles)
    right_ok = jnp.logical_and(t != nl_tiles - 1, t != nt_tiles - 1)
    xext[pl.ds(0, HALO), :] = jnp.where(left_ok, xl_ref[...], 0.0)
    xext[pl.ds(HALO, TM), :] = xm_ref[...]
    xext[pl.ds(HALO + TM, HALO), :] = jnp.where(right_ok, xr_ref[...], 0.0)
    y = jnp.zeros((TM, 3 * 256), F32)
    for j in range(GD_CONV):
        y = y + xext[pl.ds(HALO - GD_CONV // 2 + j, TM), :] * cw_ref[pl.ds(j, 1), :]
    y = _silu(y)
    qv = y[:, 0:256]
    kv = y[:, 256:512]
    q = qv * lax.rsqrt(_group_sum(qv * qv, gg_ref) + EPS) * (GD_HD ** -0.5)
    k = kv * lax.rsqrt(_group_sum(kv * kv, gg_ref) + EPS)
    v = y[:, 512:768]
    ab = ab_ref[...]
    lane = lax.broadcasted_iota(I32, (1, LANES), 1)
    gval = nea_ref[...] * _softplus(ab + dtb_ref[...])
    g1, g2, g3 = _split3(jnp.where(lane < 2 * GD_HEADS, gval, _sigmoid(ab)))
    src3 = jnp.concatenate([g1, g2, g3], axis=1)

    def mm3(xp, yp):
        xh, xl = _split2(xp)
        yh, yl = _split2(yp)
        bdh = _block_diag(yh, bm_ref)
        lhs = jnp.concatenate([xh, xl, xh], axis=1)
        rhs = jnp.concatenate([bdh, bdh, _block_diag(yl, bm_ref)], axis=0)
        return _dot(lhs, rhs)

    chains = [(d, c) for d in range(2) for c in range(nchunk)]
    each = lambda f, *lists: [f(*a) for a in zip(*lists)]
    rows = [slice(c * GD_CHUNK, (c + 1) * GD_CHUNK) for _, c in chains]
    gexp = [_dot(src3, eg_ref[d]) for d in range(2)]
    bexp = [_dot(src3, eb_ref[d]) for d in range(2)]
    qc = [q[r] for r in rows]
    kc = [k[r] for r in rows]
    vc = [v[r] for r in rows]
    gch = [gexp[d][r] for (d, _), r in zip(chains, rows)]
    bx = [bexp[d][r] for (d, _), r in zip(chains, rows)]
    mask = lambda m: [pmask_ref[d, m] for d, _ in chains]
    cums = [_dot(lmat_ref[d], jnp.concatenate(_split3(g), axis=0)) for (d, _), g in zip(chains, gch)]
    gcx = [t[0:GD_CHUNK] for t in cums]
    rest = [t[GD_CHUNK:2 * GD_CHUNK] for t in cums]
    tot = [t[2 * GD_CHUNK:3 * GD_CHUNK] for t in cums]
    dif = [_dot(lmat2_ref[d], jnp.concatenate(_split3(g * a), axis=0))
           for (d, _), g, a in zip(chains, gch, mask(2))]
    dec = each(lambda m, t: m * jnp.exp(t), mask(0), dif)
    egc = [jnp.exp(t) for t in gcx]
    kb = [t.astype(BF16) for t in kc]
    kq = each(lambda a, b: _dot_nt(jnp.concatenate([a, b.astype(BF16)], axis=0), _block_diag(a, bm_ref)), kb, qc)
    npk = each(lambda b, t, dd, m: b * t[0:GD_CHUNK] * dd * m, bx, kq, dec, mask(1))
    qk = each(lambda t, dd: t[GD_CHUNK:] * dd, kq, dec)
    n8 = [t * imask_ref[0] for t in npk]
    n8sq = each(mm3, n8, n8)
    pinv = each(lambda e, t: e - t, mask(3), n8)
    pinv = each(lambda p, m: p + mm3(p, m), pinv, n8sq)
    n8q4 = each(mm3, n8sq, n8sq)
    pinv = each(lambda p, m: p + mm3(p, m), pinv, n8q4)
    for lvl in range(1, 4):
        pc = each(lambda p, t: mm3(p, t * imask_ref[lvl]), pinv, npk)
        pinv = each(lambda p, t: p - mm3(t, p), pinv, pc)
    u = each(lambda p, a, b: mm3(p, a * b), pinv, vc, bx)
    w = each(lambda p, a, b, e: mm3(p, a * b * e), pinv, kc, bx, egc)
    for i, (d, c) in enumerate(chains):
        u_o[d, rows[i], :] = u[i]
        w_o[d, rows[i], :] = w[i].astype(BF16)
        qk_o[d, rows[i], :] = qk[i].astype(BF16)
        qd_o[d, rows[i], :] = (qc[i] * egc[i]).astype(BF16)
        kd_o[d, rows[i], :] = (kc[i] * jnp.exp(rest[i])).astype(BF16)
        eg_o[d, c * SUBLANES:(c + 1) * SUBLANES, :] = jnp.exp(tot[i][0:SUBLANES])


def _gdn_local(gdqkv, gdab, consts, nl_tiles):
    b, ntok, _ = gdqkv.shape
    nt = ntok // TM
    (cw, nea, dtb, eg, eb, lmat, lmat2, pmask, imask, gg64, bm256) = consts
    hpt = TM // HALO
    nhalo = ntok // HALO
    full = lambda a: pl.BlockSpec(a.shape, lambda i, t: (0,) * a.ndim)
    out = lambda rows: pl.BlockSpec((2, None, rows, 256), lambda i, t: (0, i, t, 0))
    kern = functools.partial(_gdn_local_kernel, nl_tiles=nl_tiles, nt_tiles=nt)
    nchunk = TM // GD_CHUNK
    return pl.pallas_call(
        kern,
        grid=(b, nt),
        in_specs=[pl.BlockSpec((None, TM, 768), lambda i, t: (i, t, 0)),
                  pl.BlockSpec((None, HALO, 768), lambda i, t: (i, jnp.maximum(t * hpt - 1, 0), 0)),
                  pl.BlockSpec((None, HALO, 768), lambda i, t: (i, jnp.minimum((t + 1) * hpt, nhalo - 1), 0)),
                  pl.BlockSpec((None, TM, LANES), lambda i, t: (i, t, 0)),
                  full(cw), full(nea), full(dtb), full(eg), full(eb), full(lmat), full(lmat2), full(pmask),
                  full(imask), full(gg64), full(bm256)],
        out_specs=[out(TM)] * 5 + [out(nchunk * SUBLANES)],
        out_shape=[jax.ShapeDtypeStruct((2, b, ntok, 256), F32)] +
                  [jax.ShapeDtypeStruct((2, b, ntok, 256), BF16)] * 4 +
                  [jax.ShapeDtypeStruct((2, b, nt * nchunk * SUBLANES, 256), F32)],
        scratch_shapes=[pltpu.VMEM((TM + 2 * HALO, 768), F32)],
        compiler_params=_params(("parallel", "parallel")),
        name="gdn_local",
    )(gdqkv, gdqkv, gdqkv, gdab, cw, nea, dtb, eg, eb, lmat, lmat2, pmask, imask, gg64, bm256)


def _gdn_scan_kernel(*refs):
    ins = (refs[0:6], refs[6:12])
    bm_ref, outs, s_ref = refs[12], refs[13:15], refs[15]
    j = pl.program_id(1)
    nchunk = TM // GD_CHUNK

    @pl.when(j == 0)
    def _init():
        s_ref[...] = jnp.zeros_like(s_ref)

    lane = lax.broadcasted_iota(I32, (1, 256), 1)
    s = [s_ref[0], s_ref[1]]
    dirs = range(2)
    for i in range(nchunk):
        cs = (i, nchunk - 1 - i)
        rows = [slice(c * GD_CHUNK, (c + 1) * GD_CHUNK) for c in cs]
        ld = lambda m: [ins[d][m][rows[d], :] for d in dirs]
        u, w, qk, qd, kd = ld(0), ld(1), ld(2), ld(3), ld(4)
        bds = [_block_diag(s[d].astype(BF16), bm_ref) for d in dirs]
        ws_qs = [_dot(jnp.concatenate([w[d], qd[d]], axis=0), bds[d]) for d in dirs]
        vb = [(u[d] - ws_qs[d][0:GD_CHUNK]).astype(BF16) for d in dirs]
        intra = [_dot(qk[d], _block_diag(vb[d], bm_ref)) for d in dirs]
        z = [_dot_tn(kd[d], vb[d]) for d in dirs]
        for d in dirs:
            outs[d][rows[d], :] = ws_qs[d][GD_CHUNK:] + intra[d]
            snew = s[d] * ins[d][5][cs[d] * SUBLANES:cs[d] * SUBLANES + 1, :]
            for h in range(GD_HEADS):
                snew = snew + jnp.where(lane // GD_HD == h, z[d][h * GD_HD:(h + 1) * GD_HD], 0.0)
            s[d] = snew
    s_ref[0] = s[0]
    s_ref[1] = s[1]


def _gdn_scan(u, w, qk, qd, kd, eg, bm256, nl_tiles):
    _, b, ntok, _ = u.shape
    nt = ntok // TM
    nc_tiles = nt - nl_tiles
    nchunk = TM // GD_CHUNK

    def tile_index(d, j):
        is_ctx = j < nc_tiles
        if d == 0:
            return jnp.where(is_ctx, nl_tiles + j, j - nc_tiles)
        return jnp.where(is_ctx, nl_tiles + nc_tiles - 1 - j, nl_tiles - 1 - (j - nc_tiles))

    tile = lambda d: pl.BlockSpec((None, None, TM, 256), lambda i, j: (d, i, tile_index(d, j), 0))
    egs = lambda d: pl.BlockSpec((None, None, nchunk * SUBLANES, 256), lambda i, j: (d, i, tile_index(d, j), 0))
    otile = lambda d: pl.BlockSpec((None, TM, 256), lambda i, j: (i, tile_index(d, j), 0))
    per_dir = lambda d: [tile(d)] * 5 + [egs(d)]
    return pl.pallas_call(
        _gdn_scan_kernel,
        grid=(b, nt),
        in_specs=per_dir(0) + per_dir(1) + [pl.BlockSpec(bm256.shape, lambda i, j: (0, 0))],
        out_specs=[otile(0), otile(1)],
        out_shape=[jax.ShapeDtypeStruct((b, ntok, 256), F32)] * 2,
        scratch_shapes=[pltpu.VMEM((2, GD_HD, 256), F32)],
        compiler_params=_params(("parallel", "arbitrary")),
        name="gdn_scan",
    )(u, w, qk, qd, kd, eg, u, w, qk, qd, kd, eg, bm256)


def _merge_kernel(x_ref, mod_ref, g1_ref, g2_ref, wg_ref, bg_ref, oa_ref, ob_ref, om_ref, of_ref, ob2_ref, gate_ref,
                  gdog_ref, gg_ref, wbr_ref, wo_ref, wrh_ref, wrl_ref, br_ref,
                  xn_o, h2_o, idx_o, gates_o):
    x = x_ref[...]
    mod = mod_ref[...]
    dm = D_MODEL
    sh1, sc1, gt1 = mod[:, 0:dm], mod[:, dm:2 * dm], mod[:, 2 * dm:3 * dm]
    sh2, sc2 = mod[:, 3 * dm:4 * dm], mod[:, 4 * dm:5 * dm]
    hb = (_rms_rows(x, g1_ref[...]) * (1.0 + sc1) + sh1).astype(BF16)

    o = of_ref[...] + ob2_ref[...]
    ss = _group_sum(o * o, gg_ref)
    od = (o * lax.rsqrt(ss * (1.0 / GD_HD) + EPS) * gdog_ref[...] * _silu(gate_ref[...])).astype(BF16)

    acc = jnp.zeros((TM, dm), F32)
    for n, br in enumerate((oa_ref[...], ob_ref[...], om_ref[...], od)):
        gpre = _dot(hb, wg_ref[:, n * dm:(n + 1) * dm]) + bg_ref[:, n * dm:(n + 1) * dm]
        acc = acc + _sigmoid(gpre) * _dot(br, wbr_ref[n])
    y = _dot(acc.astype(BF16), wo_ref[...])
    xn = x + gt1 * y
    xn_o[...] = xn
    h2 = _rms_rows(xn, g2_ref[...]) * (1.0 + sc2) + sh2
    for s in range(ROW_TILES):
        h2_o[pl.ds(s, TM, stride=ROW_TILES), :] = h2[:, s * LANES:(s + 1) * LANES]

    hh, hl = _split2(h2)
    logits = _dot(hh, wrh_ref[...]) + _dot(hl, wrh_ref[...]) + _dot(hh, wrl_ref[...]) + br_ref[...]
    lane = lax.broadcasted_iota(I32, logits.shape, 1)
    vals = jnp.full(logits.shape, NEG_BIG, F32)
    idxs = jnp.zeros(logits.shape, I32)
    l = logits
    for k in range(TOP_K):
        m = jnp.max(l, axis=-1, keepdims=True)
        i = jnp.min(jnp.where(l == m, lane, LANES), axis=-1, keepdims=True)
        vals = jnp.where(lane == k, m, vals)
        idxs = jnp.where(lane == k, i, idxs)
        l = jnp.where(lane == i, NEG_BIG * 2.0, l)
    e = jnp.exp(vals - jnp.max(vals, axis=-1, keepdims=True))
    e = jnp.where(lane < TOP_K, e, 0.0)
    gates_o[...] = e / jnp.sum(e, axis=-1, keepdims=True)
    idx_o[...] = idxs


def _merge(x_all, modsel, g1, g2, wg, bg, oa, ob, om, o2, gate, gdog, gg64, wbr, wo, wrh, wrl, brp, n_tiles, nl_tiles):
    b = x_all.shape[0]
    ntok_out = n_tiles * TM
    tok = lambda w: pl.BlockSpec((None, TM, w), lambda i, t: (i, t, 0))
    full = lambda a: pl.BlockSpec(a.shape, lambda i, t: (0,) * a.ndim)
    return pl.pallas_call(
        _merge_kernel,
        grid=(b, n_tiles),
        in_specs=[tok(D_MODEL),
                  pl.BlockSpec((None, None, 1, 6 * D_MODEL), lambda i, t: (i, t // nl_tiles, 0, 0)),
                  full(g1), full(g2), full(wg), full(bg), tok(256), tok(256), tok(256), tok(256), tok(256), tok(256),
                  full(gdog), full(gg64), full(wbr), full(wo), full(wrh), full(wrl), full(brp)],
        out_specs=[tok(D_MODEL), pl.BlockSpec((None, TM * ROW_TILES, LANES), lambda i, t: (i, t, 0)),
                   tok(LANES), tok(LANES)],
        out_shape=[jax.ShapeDtypeStruct((b, ntok_out, D_MODEL), F32),
                   jax.ShapeDtypeStruct((b, ntok_out * ROW_TILES, LANES), F32),
                   jax.ShapeDtypeStruct((b, ntok_out, LANES), I32), jax.ShapeDtypeStruct((b, ntok_out, LANES), F32)],
        compiler_params=_params(("parallel", "parallel")),
        name="merge",
    )(x_all, modsel, g1, g2, wg, bg, oa, ob, om, o2[0], o2[1], gate, gdog, gg64, wbr, wo, wrh, wrl, brp)


def _expert_kernel(bexp_ref, src0_ref, src_ref, dst_ref, tok_hbm, wup_ref, bup_ref, wdn_ref, bdn_ref, out_hbm,
                   x0, x1, x2, x3, y0, y1, y2, y3, gsem, ssem):
    s = pl.program_id(0)
    ns = pl.num_programs(0)
    xs = (x0, x1, x2, x3)
    ys = (y0, y1, y2, y3)

    def tile_rows(first):
        return pl.ds(pl.multiple_of(first, ROW_TILES), ROW_TILES)

    def issue_gather(idx_ref, blk, k):
        for r in range(EXP_SUB):
            pltpu.make_async_copy(tok_hbm.at[tile_rows(idx_ref[blk, r]), :], xs[k].at[pl.ds(r * ROW_TILES, ROW_TILES), :],
                                  gsem.at[k]).start(priority=r % DMA_PRIORITIES)

    def wait_gather(k):
        pltpu.make_async_copy(tok_hbm.at[pl.ds(0, EXP_SUB * ROW_TILES), :], xs[k], gsem.at[k]).wait()

    def issue_scatter(k):
        for r in range(EXP_SUB):
            pltpu.make_async_copy(ys[k].at[pl.ds(r * ROW_TILES, ROW_TILES), :], out_hbm.at[tile_rows(dst_ref[k, r]), :],
                                  ssem.at[k]).start(priority=r % DMA_PRIORITIES)

    def wait_scatter(k):
        pltpu.make_async_copy(ys[k], out_hbm.at[pl.ds(0, EXP_SUB * ROW_TILES), :], ssem.at[k]).wait()

    @pl.when(s == 0)
    def _prologue():
        issue_gather(src0_ref, 0, 0)
        issue_gather(src0_ref, 1, 1)

    for k in range(EXP_NSUB):
        wait_gather(k)

        @pl.when(s > 0)
        def _free_y():
            wait_scatter(k)

        if k >= 1:
            issue_scatter(k - 1)
        issue_gather(src_ref, k, (k + 2) % EXP_NSUB)
        xb = jnp.concatenate([xs[k][pl.ds(t, EXP_SUB, stride=ROW_TILES), :].astype(BF16) for t in range(ROW_TILES)],
                             axis=1)
        z = _dot(xb, wup_ref[...]) + bup_ref[...]
        glu = jnp.minimum(z[:, :D_FF], SWIGLU_LIMIT)
        lin = jnp.clip(z[:, D_FF:], -SWIGLU_LIMIT, SWIGLU_LIMIT)
        act = glu * _sigmoid(SWIGLU_ALPHA * glu) * (lin + 1.0)
        y = _dot(act.astype(BF16), wdn_ref[...]) + bdn_ref[...]
        for t in range(ROW_TILES):
            ys[k][pl.ds(t, EXP_SUB, stride=ROW_TILES), :] = y[:, t * LANES:(t + 1) * LANES]
    issue_scatter(EXP_NSUB - 1)

    @pl.when(s == ns - 1)
    def _epilogue():
        for k in range(EXP_NSUB):
            wait_scatter(k)
        wait_gather(0)
        wait_gather(1)


def _experts(blk_exp, row_src, row_dst, tok, wup, bup, wdn, bdn):
    ns = blk_exp.shape[0]
    n_rows = ns * EXP_TILE
    src = jnp.concatenate([row_src, jnp.zeros((2 * EXP_SUB,), I32)])
    src0 = src[:2 * EXP_SUB].reshape(2, EXP_SUB)
    src_ahead = jnp.concatenate([src[2 * EXP_SUB:], jnp.zeros((2 * EXP_SUB,), I32)])[:n_rows]
    idx = lambda: pl.BlockSpec((None, EXP_NSUB, EXP_SUB), lambda i, be: (i, 0, 0), memory_space=pltpu.SMEM)
    gs = pltpu.PrefetchScalarGridSpec(
        num_scalar_prefetch=1,
        grid=(ns,),
        in_specs=[pl.BlockSpec(memory_space=pltpu.SMEM),
                  idx(), idx(),
                  pl.BlockSpec(memory_space=pl.ANY),
                  pl.BlockSpec((None, D_MODEL, 2 * D_FF), lambda i, be: (be[i], 0, 0)),
                  pl.BlockSpec((None, 1, 2 * D_FF), lambda i, be: (be[i], 0, 0)),
                  pl.BlockSpec((None, D_FF, D_MODEL), lambda i, be: (be[i], 0, 0)),
                  pl.BlockSpec((None, 1, D_MODEL), lambda i, be: (be[i], 0, 0))],
        out_specs=pl.BlockSpec(memory_space=pl.ANY),
        scratch_shapes=[pltpu.VMEM((EXP_SUB * ROW_TILES, LANES), F32)] * (2 * EXP_NSUB) +
                       [pltpu.SemaphoreType.DMA((EXP_NSUB,)), pltpu.SemaphoreType.DMA((EXP_NSUB,))],
    )
    return pl.pallas_call(
        _expert_kernel,
        grid_spec=gs,
        out_shape=jax.ShapeDtypeStruct((n_rows * ROW_TILES, LANES), F32),
        compiler_params=_params(("arbitrary",)),
        name="experts",
    )(blk_exp, src0, src_ahead.reshape(ns, EXP_NSUB, EXP_SUB), row_dst.reshape(ns, EXP_NSUB, EXP_SUB),
      tok, wup, bup, wdn, bdn)


def _combine_kernel(x_ref, mod_ref, gates_ref, y0_ref, y1_ref, y2_ref, y3_ref, o_ref):
    g2 = mod_ref[...][:, 5 * D_MODEL:6 * D_MODEL]
    gates = gates_ref[...]
    for s in range(ROW_TILES):
        cols = slice(s * LANES, (s + 1) * LANES)
        f = jnp.zeros((TM, LANES), F32)
        for k, y_ref in enumerate((y0_ref, y1_ref, y2_ref, y3_ref)):
            f = f + gates[:, k:k + 1] * y_ref[pl.ds(s, TM, stride=ROW_TILES), :]
        o_ref[:, cols] = x_ref[:, cols] + g2[:, cols] * f


def _combine(xn, modsel, gates, out4, nl_tiles):
    b, ntok, _ = xn.shape
    nt = ntok // TM
    slab = b * nt
    tok = lambda w: pl.BlockSpec((None, TM, w), lambda i, t: (i, t, 0))
    ys = lambda k: pl.BlockSpec((TM * ROW_TILES, LANES), lambda i, t: (k * slab + i * nt + t, 0))
    return pl.pallas_call(
        _combine_kernel,
        grid=(b, nt),
        in_specs=[tok(D_MODEL),
                  pl.BlockSpec((None, None, 1, 6 * D_MODEL), lambda i, t: (i, t // nl_tiles, 0, 0)),
                  tok(LANES), ys(0), ys(1), ys(2), ys(3)],
        out_specs=tok(D_MODEL),
        out_shape=jax.ShapeDtypeStruct((b, ntok, D_MODEL), F32),
        compiler_params=_params(("parallel", "parallel")),
        name="combine",
    )(xn, modsel, gates, out4, out4, out4, out4)


def _np_consts():
    lane = np.arange(256)
    half, r = lane // 128, lane % 128
    s, j = r // 16, r % 16
    d = np.where(j < 8, j + 8 * half, 16 + (j - 8) + 8 * half)
    perm_da = s * 32 + d
    dperm_da = d
    h, j = r // 32, r % 32
    d = np.where(j < 16, j + 16 * half, 32 + (j - 16) + 16 * half)
    perm_gq = h * 64 + d
    perm_gk = (h // 2) * 64 + d
    dperm_ga = d
    hv, dv = lane // 64, lane % 64
    perm_gv = (hv // 2) * 64 + dv
    grp = lambda key: (key[:, None] == key[None, :]).astype(np.float32)
    gda = grp((lane % 128) // 16)
    gga = grp((lane % 128) // 32)
    g64 = grp(lane // 64)
    stack2 = lambda g: np.concatenate([g, g], axis=0)
    bm256 = (np.arange(256)[:, None] // 64 == lane[None, :] // 64).astype(np.float32)
    bm512 = (np.arange(512)[:, None] // 128 == lane[None, :] // 64).astype(np.float32)
    i = np.arange(64)
    jj = lane % 64
    lmat, lmat2, pmask = [], [], []
    for dirn in range(2):
        if dirn == 0:
            inc = (i[None, :] <= i[:, None]); aft = (i[None, :] > i[:, None])
            p_incl = (i[:, None] >= jj[None, :]); p_strict = (i[:, None] > jj[None, :]); p_after = (i[:, None] > jj[None, :])
        else:
            inc = (i[None, :] >= i[:, None]); aft = (i[None, :] < i[:, None])
            p_incl = (i[:, None] <= jj[None, :]); p_strict = (i[:, None] < jj[None, :]); p_after = (i[:, None] < jj[None, :])
        ones = np.ones((64, 64))
        l1 = np.concatenate([inc, aft, ones], axis=0).astype(np.float32)
        lmat.append(np.concatenate([l1, l1, l1], axis=1))
        l2 = inc.astype(np.float32)
        lmat2.append(np.concatenate([l2, l2, l2], axis=1))
        eye = (i[:, None] == jj[None, :])
        pmask.append(np.stack([p_incl, p_strict, p_after, eye]).astype(np.float32))
    bi, bj = i[:, None], jj[None, :]
    imask = np.stack([bi // 8 == bj // 8,
                      (bi // 16 == bj // 16) & (bi // 8 != bj // 8),
                      (bi // 32 == bj // 32) & (bi // 16 != bj // 16),
                      bi // 32 != bj // 32]).astype(np.float32)
    eg = np.zeros((2, 128, 256), np.float32)
    eb = np.zeros((2, 128, 256), np.float32)
    for dirn in range(2):
        for hh in range(GD_HEADS):
            eg[dirn, dirn * GD_HEADS + hh, hh * 64:(hh + 1) * 64] = 1.0
            eb[dirn, 2 * GD_HEADS + dirn * GD_HEADS + hh, hh * 64:(hh + 1) * 64] = 1.0
    eg = np.concatenate([eg, eg, eg], axis=1)
    eb = np.concatenate([eb, eb, eb], axis=1)
    return dict(perm_da=perm_da, dperm_da=dperm_da, perm_gq=perm_gq, perm_gk=perm_gk, dperm_ga=dperm_ga, perm_gv=perm_gv,
                gda=stack2(gda), gga=stack2(gga), g64=stack2(g64), bm256=bm256, bm512=bm512,
                lmat=np.stack(lmat), lmat2=np.stack(lmat2), pmask=np.stack(pmask), imask=imask, eg=eg, eb=eb)


def _rope_tables(seq, ctx_len, quarter, reps):
    t = jnp.arange(seq)
    row = (t // GRID_W).astype(F32)
    col = (t % GRID_W).astype(F32)
    inv = ROPE_THETA ** (-jnp.arange(quarter, dtype=F32) / quarter)
    ang = jnp.concatenate([row[:, None] * inv, col[:, None] * inv], axis=1)
    ang = jnp.tile(ang, (1, reps))
    cos = jnp.concatenate([jnp.cos(ang), jnp.ones((ctx_len, LANES), F32)], axis=0)
    sin = jnp.concatenate([jnp.sin(ang), jnp.zeros((ctx_len, LANES), F32)], axis=0)
    return cos, sin


def _routing(idx, n_tok):
    n_asg = n_tok * TOP_K
    e_flat = idx.reshape(-1)
    order = jnp.argsort(e_flat, stable=True).astype(I32)
    counts = jnp.sum((e_flat[:, None] == jnp.arange(N_EXPERTS, dtype=I32)[None, :]).astype(I32), axis=0)
    padded = (counts + EXP_TILE - 1) // EXP_TILE * EXP_TILE
    start = jnp.cumsum(counts) - counts
    pend = jnp.cumsum(padded)
    pstart = pend - padded
    n_tiles = (n_asg + N_EXPERTS * (EXP_TILE - 1)) // EXP_TILE + 1
    n_rows = n_tiles * EXP_TILE
    tile_row0 = jnp.arange(n_tiles, dtype=I32) * EXP_TILE
    tile_exp = jnp.minimum(jnp.sum((pend[None, :] <= tile_row0[:, None]).astype(I32), axis=1), N_EXPERTS - 1)
    row = jnp.arange(n_rows, dtype=I32)
    e_row = jnp.repeat(tile_exp, EXP_TILE)
    local = row - pstart[e_row]
    valid = local < counts[e_row]
    asg = order[jnp.minimum(start[e_row] + local, n_asg - 1)]
    tok_of = asg // TOP_K
    row_src = jnp.where(valid, tok_of, 0)
    n_pad_before = row - (start[e_row] + counts[e_row])
    row_dst = jnp.where(valid, (asg % TOP_K) * n_tok + tok_of, n_asg + n_pad_before)
    return tile_exp.astype(I32), (row_src * ROW_TILES).astype(I32), (row_dst * ROW_TILES).astype(I32)


def kernel(x, c, ctx, c_ctx, w_mod, b_mod, norm1_g, norm2_g, w_in, b_gate, da_q_g, da_k_g, da_lam_q1, da_lam_k1,
           da_lam_q2, da_lam_k2, da_sub_g, ga_q_g, ga_k_g, gm_v_g, gm_ws, gm_bs, gd_conv_w, gd_a_log, gd_dt_bias,
           gd_out_g, w_br, w_o, w_router, b_router, w_up, b_up, w_down, b_down):
    bsz, seq, dm = x.shape
    ctx_len = ctx.shape[1]
    depth = w_mod.shape[0]
    assert dm == D_MODEL and seq % TM == 0 and ctx_len % TM == 0 and seq % ctx_len == 0 and seq % GRID_W == 0
    ntok = seq + ctx_len
    nl_tiles = seq // TM
    nt_tiles = ntok // TM
    cn = _np_consts()
    bfc = lambda a: jnp.asarray(a, BF16)
    gda, gga, g64 = bfc(cn['gda']), bfc(cn['gga']), bfc(cn['g64'])
    bm256, bm512 = bfc(cn['bm256']), bfc(cn['bm512'])
    lmat, lmat2 = bfc(cn['lmat']), bfc(cn['lmat2'])
    pmask = jnp.asarray(cn['pmask'], F32)
    imask = jnp.asarray(cn['imask'], F32)
    eg, eb = bfc(cn['eg']), bfc(cn['eb'])
    cosa, sina = _rope_tables(seq, ctx_len, DA_HD // 4, LANES // (DA_HD // 2))
    cosg, sing = _rope_tables(seq, ctx_len, GA_HD // 4, LANES // (GA_HD // 2))
    cuts = np.cumsum((256, 256, 256, 256, 128, 128, 256, 256, 768, 256, 8, 8))
    mrows = -(-(bsz + 1) // SUBLANES) * SUBLANES
    cpad = jnp.zeros((mrows, dm), F32).at[:bsz].set(c).at[bsz].set(c_ctx)

    x_all = jnp.concatenate([x, ctx], axis=1)
    for l in range(depth):
        need_ctx = l < depth - 1
        lam_init = 0.8 - 0.6 * math.exp(-0.3 * l)
        mod = _adaln(cpad, w_mod[l], b_mod[l])
        modsel = jnp.stack([mod[:bsz], jnp.broadcast_to(mod[bsz], (bsz, 6 * dm))], axis=1).reshape(bsz, 2, 1, 6 * dm)

        wl = w_in[l]
        aq, ak, av, gq, gk, gv, mu, mv, dqkv, dgate, dda, ddb = jnp.split(wl[:, :cuts[-1]], cuts[:-1].tolist(), axis=1)
        wcat = jnp.concatenate([aq[:, cn['perm_da']], ak[:, cn['perm_da']], av, gq[:, cn['perm_gq']],
                                gk[:, cn['perm_gk']], gv[:, cn['perm_gv']], mu, mv, dqkv, dgate, dda, ddb,
                                jnp.zeros((dm, LANES - 4 * GD_HEADS), F32)], axis=1).astype(BF16)
        row = lambda a: a.reshape(1, -1).astype(F32)
        daqg = row(da_q_g[l][cn['dperm_da']])
        dakg = row(da_k_g[l][cn['dperm_da']])
        gaqg = row(ga_q_g[l][cn['dperm_ga']])
        gakg = row(ga_k_g[l][cn['dperm_ga']])
        wsp = gm_ws[l].transpose(1, 0, 2).reshape(GM_CHUNK, GM_GROUPS * GM_CHUNK).astype(BF16)
        bsp = jnp.repeat(gm_bs[l].T, GM_GW, axis=1)
        consts = (gda, gga, cosa, sina, cosg, sing, daqg, dakg, gaqg, gakg, row(gm_v_g[l]), wsp, bsp, bm512)
        daq, dak, dav, gaq, gak, gav, om, gdqkv, gdgate, gdab = _inproj(
            x_all, modsel, row(norm1_g[l]), wcat, consts, nl_tiles)

        lam = (jnp.exp(jnp.sum(da_lam_q1[l] * da_lam_k1[l])) - jnp.exp(jnp.sum(da_lam_q2[l] * da_lam_k2[l]))
               + lam_init).reshape(1).astype(F32)
        subg = row(jnp.tile(da_sub_g[l], DA_HEADS))
        att = functools.partial(_attention, lam, gg64=g64, subg=subg)
        lat = dict(q_start=0, q_len=seq, kv_start=0, kv_len=ntok)
        oa = att(daq, dak, dav, diff=True, post_scale=1.0 - lam_init, **lat)
        ob = att(gaq, gak, gav, diff=False, post_scale=1.0, **lat)
        if need_ctx:
            cx = dict(q_start=seq, q_len=ctx_len, kv_start=seq, kv_len=ctx_len)
            oa = jnp.concatenate([oa, att(daq, dak, dav, diff=True, post_scale=1.0 - lam_init, **cx)], axis=1)
            ob = jnp.concatenate([ob, att(gaq, gak, gav, diff=False, post_scale=1.0, **cx)], axis=1)

        nea = jnp.zeros((1, LANES), F32).at[0, :2 * GD_HEADS].set(-jnp.exp(gd_a_log[l].reshape(-1)))
        dtb = jnp.zeros((1, LANES), F32).at[0, :2 * GD_HEADS].set(gd_dt_bias[l].reshape(-1))
        gconsts = (gd_conv_w[l], nea, dtb, eg, eb, lmat, lmat2, pmask, imask, g64, bm256)
        u, w, qk, qd, kd, egl = _gdn_local(gdqkv, gdab, gconsts, nl_tiles)
        o2 = _gdn_scan(u, w, qk, qd, kd, egl, bm256, nl_tiles)

        n_tiles = nt_tiles if need_ctx else nl_tiles
        wg = wl[:, cuts[-1]:].astype(BF16)
        wr = jnp.zeros((dm, LANES), F32).at[:, :N_EXPERTS].set(w_router[l])
        wrh = wr.astype(BF16)
        wrl = (wr - wrh.astype(F32)).astype(BF16)
        brp = jnp.full((1, LANES), NEG_BIG, F32).at[0, :N_EXPERTS].set(b_router[l])
        xn, h2, idx, gates = _merge(x_all, modsel, row(norm1_g[l]), row(norm2_g[l]), wg, row(b_gate[l]),
                                    oa, ob, om, o2, gdgate, row(jnp.tile(gd_out_g[l], GD_HEADS)), g64,
                                    w_br[l].astype(BF16), w_o[l].astype(BF16), wrh, wrl, brp, n_tiles, nl_tiles)

        n_tok = bsz * n_tiles * TM
        tile_exp, row_src, row_dst = _routing(idx[:, :, :TOP_K], n_tok)
        out4 = _experts(tile_exp, row_src, row_dst, h2.reshape(n_tok * ROW_TILES, LANES), w_up[l].astype(BF16),
                        b_up[l].reshape(N_EXPERTS, 1, 2 * D_FF), w_down[l].astype(BF16),
                        b_down[l].reshape(N_EXPERTS, 1, dm))
        x_all = _combine(xn, modsel, gates, out4, nl_tiles)
    return x_all[:, :seq] if x_all.shape[1] != seq else x_all
```

```python
import functools
import math

import numpy as np
import jax
import jax.numpy as jnp
from jax import lax
from jax.experimental import pallas as pl
from jax.experimental.pallas import tpu as pltpu

F32 = jnp.float32
BF16 = jnp.bfloat16
I32 = jnp.int32

D_MODEL = 1024
GRID_W = 64
EPS = 1e-6
ROPE_THETA = 10000.0
DA_HEADS, DA_HD = 4, 32
GA_HEADS, GA_KV, GA_HD = 4, 2, 64
GM_GROUPS, GM_GW, GM_CHUNK = 4, 64, 128
GD_HEADS, GD_HD, GD_CONV, GD_CHUNK = 4, 64, 5, 64
N_BRANCH, BR_W = 4, 256
N_EXPERTS, TOP_K, D_FF = 32, 4, 1024
SWIGLU_ALPHA, SWIGLU_LIMIT = 1.702, 7.0
EXP_SUB = 256
EXP_NSUB = 4
EXP_TILE = EXP_SUB * EXP_NSUB
DMA_PRIORITIES = 2

LANES = 128
SUBLANES = 8
TM = 256
TQ = 128
TQ_PER_STEP = 2
HALO = SUBLANES
ROW_TILES = D_MODEL // LANES
VMEM_LIMIT = 56 * 1024 * 1024
NEG_BIG = -1e30
LOG2E = math.log2(math.e)

C_DAQ, C_DAK, C_DAV, C_GAQ, C_GAK, C_GAV, C_GMU, C_GMV, C_GDQKV, C_GDGATE, C_GDAB, C_END = (
    0, 256, 512, 768, 1024, 1280, 1536, 1792, 2048, 2816, 3072, 3200)


def _params(sem):
    return pltpu.CompilerParams(dimension_semantics=sem, vmem_limit_bytes=VMEM_LIMIT)


def _split2(x):
    hi = x.astype(BF16)
    lo = (x - hi.astype(F32)).astype(BF16)
    return hi, lo


def _split3(x):
    hi = x.astype(BF16)
    r = x - hi.astype(F32)
    mid = r.astype(BF16)
    lo = (r - mid.astype(F32)).astype(BF16)
    return hi, mid, lo


def _dot(a, b):
    return jnp.dot(a, b, preferred_element_type=F32)


def _dot_nt(a, b):
    return lax.dot_general(a, b, (((1,), (1,)), ((), ())), preferred_element_type=F32)


def _dot_tn(a, b):
    return lax.dot_general(a, b, (((0,), (0,)), ((), ())), preferred_element_type=F32)


def _group_sum(x, gg_ref):
    hi, lo = _split2(x)
    return _dot(jnp.concatenate([hi, lo], axis=1), gg_ref[...])


def _sigmoid(x):
    return 1.0 / (1.0 + jnp.exp(-x))


def _silu(x):
    return x * _sigmoid(x)


def _gelu(x):
    return 0.5 * x * (1.0 + lax.erf(x * (1.0 / math.sqrt(2.0))))


def _softplus(x):
    return jnp.maximum(x, 0.0) + jnp.log1p(jnp.exp(-jnp.abs(x)))


def _rms_rows(x, gain):
    return x * lax.rsqrt(jnp.mean(x * x, axis=-1, keepdims=True) + EPS) * gain


def _block_diag(y, mask_ref):
    return jnp.concatenate([y, y, y, y], axis=0) * mask_ref[...]


def _adaln_kernel(c_ref, w_ref, b_ref, o_ref):
    a = _silu(c_ref[...])
    ah, al = _split2(a)
    w = w_ref[...]
    wh, wl = _split2(w)
    o_ref[...] = _dot(ah, wh) + _dot(al, wh) + _dot(ah, wl) + b_ref[...]


def _adaln(cpad, w_mod, b_mod):
    m = cpad.shape[0]
    n = w_mod.shape[1]
    tn = 512
    return pl.pallas_call(
        _adaln_kernel,
        grid=(n // tn,),
        in_specs=[pl.BlockSpec((m, D_MODEL), lambda j: (0, 0)),
                  pl.BlockSpec((D_MODEL, tn), lambda j: (0, j)),
                  pl.BlockSpec((1, tn), lambda j: (0, j))],
        out_specs=pl.BlockSpec((m, tn), lambda j: (0, j)),
        out_shape=jax.ShapeDtypeStruct((m, n), F32),
        compiler_params=_params(("arbitrary",)),
        name="adaln",
    )(cpad, w_mod, b_mod.reshape(1, n))


def _qk_prep(z, gg_ref, nd, gain_ref, cos_ref, sin_ref, scale):
    ss = _group_sum(z * z, gg_ref)
    y = z * lax.rsqrt(ss * (1.0 / nd) + EPS) * gain_ref[...]
    a = y[:, :LANES]
    b = y[:, LANES:]
    c = cos_ref[...]
    s = sin_ref[...]
    out = jnp.concatenate([a * c - b * s, b * c + a * s], axis=1)
    if scale != 1.0:
        out = out * scale
    return out


def _inproj_kernel(x_ref, mod_ref, g1_ref, w_ref, gda_ref, gga_ref, cosa_ref, sina_ref, cosg_ref, sing_ref,
                   daqg_ref, dakg_ref, gaqg_ref, gakg_ref, gmvg_ref, wsp_ref, bsp_ref, bm_ref,
                   daq_o, dak_o, dav_o, gaq_o, gak_o, gav_o, om_o, gdqkv_o, gdgate_o, gdab_o):
    x = x_ref[...]
    mod = mod_ref[...]
    sh = mod[:, 0:D_MODEL]
    sc = mod[:, D_MODEL:2 * D_MODEL]
    hb = (_rms_rows(x, g1_ref[...]) * (1.0 + sc) + sh).astype(BF16)

    def proj(lo, hi):
        return _dot(hb, w_ref[:, lo:hi])

    daq_o[...] = _qk_prep(proj(C_DAQ, C_DAK), gda_ref, DA_HD, daqg_ref, cosa_ref, sina_ref, LOG2E * DA_HD ** -0.5).astype(BF16)
    dak_o[...] = _qk_prep(proj(C_DAK, C_DAV), gda_ref, DA_HD, dakg_ref, cosa_ref, sina_ref, 1.0).astype(BF16)
    dav_o[...] = proj(C_DAV, C_GAQ).astype(BF16)
    gaq_o[...] = _qk_prep(proj(C_GAQ, C_GAK), gga_ref, GA_HD, gaqg_ref, cosg_ref, sing_ref, LOG2E * GA_HD ** -0.5).astype(BF16)
    gak_o[...] = _qk_prep(proj(C_GAK, C_GAV), gga_ref, GA_HD, gakg_ref, cosg_ref, sing_ref, 1.0).astype(BF16)
    gav_o[...] = proj(C_GAV, C_GMU).astype(BF16)

    u = _gelu(proj(C_GMU, C_GMV))
    v = _rms_rows(_gelu(proj(C_GMV, C_GDQKV)), gmvg_ref[...])
    for j in range(TM // GM_CHUNK):
        rows = slice(j * GM_CHUNK, (j + 1) * GM_CHUNK)
        bd = _block_diag(v[rows].astype(BF16), bm_ref)
        mixed = _dot(wsp_ref[...], bd) + bsp_ref[...]
        om_o[rows, :] = (u[rows] * mixed).astype(BF16)

    gdqkv_o[...] = proj(C_GDQKV, C_GDGATE)
    gdgate_o[...] = proj(C_GDGATE, C_GDAB)
    gdab_o[...] = proj(C_GDAB, C_END)


def _inproj(x_all, modsel, g1, wcat, consts, nl_tiles):
    b, ntok, _ = x_all.shape
    nt = ntok // TM
    tok = lambda w: pl.BlockSpec((None, TM, w), lambda i, t: (i, t, 0))
    full = lambda a: pl.BlockSpec(a.shape, lambda i, t: (0,) * a.ndim)
    tab = pl.BlockSpec((TM, LANES), lambda i, t: (t, 0))
    (gda, gga, cosa, sina, cosg, sing, daqg, dakg, gaqg, gakg, gmvg, wsp, bsp, bm512) = consts
    in_specs = [tok(D_MODEL),
                pl.BlockSpec((None, None, 1, 6 * D_MODEL), lambda i, t: (i, t // nl_tiles, 0, 0)),
                full(g1), full(wcat), full(gda), full(gga), tab, tab, tab, tab,
                full(daqg), full(dakg), full(gaqg), full(gakg), full(gmvg), full(wsp), full(bsp), full(bm512)]
    widths = [(256, BF16)] * 7 + [(768, F32), (256, F32), (128, F32)]
    return pl.pallas_call(
        _inproj_kernel,
        grid=(b, nt),
        in_specs=in_specs,
        out_specs=[tok(w) for w, _ in widths],
        out_shape=[jax.ShapeDtypeStruct((b, ntok, w), dt) for w, dt in widths],
        compiler_params=_params(("parallel", "parallel")),
        name="inproj",
    )(x_all, modsel, g1, wcat, gda, gga, cosa, sina, cosg, sing, daqg, dakg, gaqg, gakg, gmvg, wsp, bsp, bm512)


def _attn_kernel(lam_ref, q_ref, k_ref, v_ref, gg_ref, subg_ref, o_ref, *, diff, qlanes, post_scale):
    k = k_ref[...]
    v = v_ref[...]
    tq = TQ
    lane = lax.broadcasted_iota(I32, (1, 2 * LANES), 1)
    qhead = (lane % LANES) // qlanes
    vhead = lane // 64
    lam = lam_ref[0]
    nloop = DA_HEADS if diff else GA_HEADS // 2
    units = [(qt, h) for qt in range(q_ref.shape[0] // TQ) for h in range(nloop)]

    def scores(unit):
        qt, h = unit
        q = q_ref[qt * TQ:(qt + 1) * TQ, :]
        zero = jnp.zeros_like(q)
        lhs = jnp.concatenate([jnp.where(qhead == 2 * h, q, zero), jnp.where(qhead == 2 * h + 1, q, zero)], axis=0)
        return _dot_nt(lhs, k)

    def softmax(s):
        m = jnp.max(s, axis=-1, keepdims=True)
        p = jnp.exp2(s - m)
        l = jnp.sum(p, axis=-1, keepdims=True)
        return p, l

    accs = [jnp.zeros((tq, 2 * LANES), F32) for _ in range(q_ref.shape[0] // TQ)]
    s_next = scores(units[0])
    for n, (qt, h) in enumerate(units):
        s_cur = s_next
        if n + 1 < len(units):
            s_next = scores(units[n + 1])
        p, l = softmax(s_cur)
        if diff:
            a = (p[:tq] - p[tq:] * (lam * l[:tq] / l[tq:])).astype(BF16)
            accs[qt] = accs[qt] + jnp.where(vhead == h, _dot(a, v) / l[:tq], 0.0)
        else:
            o = _dot(p.astype(BF16), v) / l
            accs[qt] = accs[qt] + jnp.where(vhead == 2 * h, o[:tq], 0.0) + jnp.where(vhead == 2 * h + 1, o[tq:], 0.0)
    for qt, acc in enumerate(accs):
        if diff:
            ss = _group_sum(acc * acc, gg_ref)
            acc = acc * lax.rsqrt(ss * (1.0 / (2 * DA_HD)) + EPS) * subg_ref[...] * post_scale
        o_ref[qt * TQ:(qt + 1) * TQ, :] = acc.astype(BF16)


def _attention(lam, q, k, v, gg64, subg, *, diff, post_scale, q_start, q_len, kv_start, kv_len):
    b = q.shape[0]
    tqb = TQ * TQ_PER_STEP
    qoff = q_start // tqb
    kvoff = kv_start // kv_len
    kern = functools.partial(_attn_kernel, diff=diff, qlanes=16 if diff else 32, post_scale=post_scale)
    full = lambda a: pl.BlockSpec(a.shape, lambda i, j: (0,) * a.ndim)
    kvspec = pl.BlockSpec((None, kv_len, 256), lambda i, j: (i, kvoff, 0))
    return pl.pallas_call(
        kern,
        grid=(b, q_len // tqb),
        in_specs=[pl.BlockSpec(memory_space=pltpu.SMEM),
                  pl.BlockSpec((None, tqb, 256), lambda i, j: (i, qoff + j, 0)),
                  kvspec, kvspec, full(gg64), full(subg)],
        out_specs=pl.BlockSpec((None, tqb, 256), lambda i, j: (i, j, 0)),
        out_shape=jax.ShapeDtypeStruct((b, q_len, 256), BF16),
        compiler_params=_params(("parallel", "parallel")),
        name="diff_attn" if diff else "gqa_attn",
    )(lam, q, k, v, gg64, subg)


def _gdn_local_kernel(xm_ref, xl_ref, xr_ref, ab_ref, cw_ref, nea_ref, dtb_ref, eg_ref, eb_ref,
                      lmat_ref, lmat2_ref, pmask_ref, imask_ref, gg_ref, bm_ref,
                      u_o, w_o, qk_o, qd_o, kd_o, eg_o, xext, *, nl_tiles, nt_tiles):
    t = pl.program_id(1)
    nchunk = TM // GD_CHUNK

    left_ok = jnp.logical_and(t != 0, t != nl_tiles)
    right_ok = jnp.logical_and(t != nl_tiles - 1, t != nt_tiles - 1)
    xext[pl.ds(0, HALO), :] = jnp.where(left_ok, xl_ref[...], 0.0)
    xext[pl.ds(HALO, TM), :] = xm_ref[...]
    xext[pl.ds(HALO + TM, HALO), :] = jnp.where(right_ok, xr_ref[...], 0.0)
    y = jnp.zeros((TM, 3 * 256), F32)
    for j in range(GD_CONV):
        y = y + xext[pl.ds(HALO - GD_CONV // 2 + j, TM), :] * cw_ref[pl.ds(j, 1), :]
    y = _silu(y)
    qv = y[:, 0:256]
    kv = y[:, 256:512]
    q = qv * lax.rsqrt(_group_sum(qv * qv, gg_ref) + EPS) * (GD_HD ** -0.5)
    k = kv * lax.rsqrt(_group_sum(kv * kv, gg_ref) + EPS)
    v = y[:, 512:768]
    ab = ab_ref[...]
    lane = lax.broadcasted_iota(I32, (1, LANES), 1)
    gval = nea_ref[...] * _softplus(ab + dtb_ref[...])
    g1, g2, g3 = _split3(jnp.where(lane < 2 * GD_HEADS, gval, _sigmoid(ab)))
    src3 = jnp.concatenate([g1, g2, g3], axis=1)

    def mm3(xp, yp):
        xh, xl = _split2(xp)
        yh, yl = _split2(yp)
        bdh = _block_diag(yh, bm_ref)
        lhs = jnp.concatenate([xh, xl, xh], axis=1)
        rhs = jnp.concatenate([bdh, bdh, _block_diag(yl, bm_ref)], axis=0)
        return _dot(lhs, rhs)

    chains = [(d, c) for d in range(2) for c in range(nchunk)]
    each = lambda f, *lists: [f(*a) for a in zip(*lists)]
    rows = [slice(c * GD_CHUNK, (c + 1) * GD_CHUNK) for _, c in chains]
    gexp = [_dot(src3, eg_ref[d]) for d in range(2)]
    bexp = [_dot(src3, eb_ref[d]) for d in range(2)]
    qc = [q[r] for r in rows]
    kc = [k[r] for r in rows]
    vc = [v[r] for r in rows]
    gch = [gexp[d][r] for (d, _), r in zip(chains, rows)]
    bx = [bexp[d][r] for (d, _), r in zip(chains, rows)]
    mask = lambda m: [pmask_ref[d, m] for d, _ in chains]
    cums = [_dot(lmat_ref[d], jnp.concatenate(_split3(g), axis=0)) for (d, _), g in zip(chains, gch)]
    gcx = [t[0:GD_CHUNK] for t in cums]
    rest = [t[GD_CHUNK:2 * GD_CHUNK] for t in cums]
    tot = [t[2 * GD_CHUNK:3 * GD_CHUNK] for t in cums]
    dif = [_dot(lmat2_ref[d], jnp.concatenate(_split3(g * a), axis=0))
           for (d, _), g, a in zip(chains, gch, mask(2))]
    dec = each(lambda m, t: m * jnp.exp(t), mask(0), dif)
    egc = [jnp.exp(t) for t in gcx]
    kb = [t.astype(BF16) for t in kc]
    kq = each(lambda a, b: _dot_nt(jnp.concatenate([a, b.astype(BF16)], axis=0), _block_diag(a, bm_ref)), kb, qc)
    npk = each(lambda b, t, dd, m: b * t[0:GD_CHUNK] * dd * m, bx, kq, dec, mask(1))
    qk = each(lambda t, dd: t[GD_CHUNK:] * dd, kq, dec)
    n8 = [t * imask_ref[0] for t in npk]
    n8sq = each(mm3, n8, n8)
    pinv = each(lambda e, t: e - t, mask(3), n8)
    pinv = each(lambda p, m: p + mm3(p, m), pinv, n8sq)
    n8q4 = each(mm3, n8sq, n8sq)
    pinv = each(lambda p, m: p + mm3(p, m), pinv, n8q4)
    for lvl in range(1, 4):
        pc = each(lambda p, t: mm3(p, t * imask_ref[lvl]), pinv, npk)
        pinv = each(lambda p, t: p - mm3(t, p), pinv, pc)
    u = each(lambda p, a, b: mm3(p, a * b), pinv, vc, bx)
    w = each(lambda p, a, b, e: mm3(p, a * b * e), pinv, kc, bx, egc)
    for i, (d, c) in enumerate(chains):
        u_o[d, rows[i], :] = u[i]
        w_o[d, rows[i], :] = w[i].astype(BF16)
        qk_o[d, rows[i], :] = qk[i].astype(BF16)
        qd_o[d, rows[i], :] = (qc[i] * egc[i]).astype(BF16)
        kd_o[d, rows[i], :] = (kc[i] * jnp.exp(rest[i])).astype(BF16)
        eg_o[d, c * SUBLANES:(c + 1) * SUBLANES, :] = jnp.exp(tot[i][0:SUBLANES])


def _gdn_local(gdqkv, gdab, consts, nl_tiles):
    b, ntok, _ = gdqkv.shape
    nt = ntok // TM
    (cw, nea, dtb, eg, eb, lmat, lmat2, pmask, imask, gg64, bm256) = consts
    hpt = TM // HALO
    nhalo = ntok // HALO
    full = lambda a: pl.BlockSpec(a.shape, lambda i, t: (0,) * a.ndim)
    out = lambda rows: pl.BlockSpec((2, None, rows, 256), lambda i, t: (0, i, t, 0))
    kern = functools.partial(_gdn_local_kernel, nl_tiles=nl_tiles, nt_tiles=nt)
    nchunk = TM // GD_CHUNK
    return pl.pallas_call(
        kern,
        grid=(b, nt),
        in_specs=[pl.BlockSpec((None, TM, 768), lambda i, t: (i, t, 0)),
                  pl.BlockSpec((None, HALO, 768), lambda i, t: (i, jnp.maximum(t * hpt - 1, 0), 0)),
                  pl.BlockSpec((None, HALO, 768), lambda i, t: (i, jnp.minimum((t + 1) * hpt, nhalo - 1), 0)),
                  pl.BlockSpec((None, TM, LANES), lambda i, t: (i, t, 0)),
                  full(cw), full(nea), full(dtb), full(eg), full(eb), full(lmat), full(lmat2), full(pmask),
                  full(imask), full(gg64), full(bm256)],
        out_specs=[out(TM)] * 5 + [out(nchunk * SUBLANES)],
        out_shape=[jax.ShapeDtypeStruct((2, b, ntok, 256), F32)] +
                  [jax.ShapeDtypeStruct((2, b, ntok, 256), BF16)] * 4 +
                  [jax.ShapeDtypeStruct((2, b, nt * nchunk * SUBLANES, 256), F32)],
        scratch_shapes=[pltpu.VMEM((TM + 2 * HALO, 768), F32)],
        compiler_params=_params(("parallel", "parallel")),
        name="gdn_local",
    )(gdqkv, gdqkv, gdqkv, gdab, cw, nea, dtb, eg, eb, lmat, lmat2, pmask, imask, gg64, bm256)


def _gdn_scan_kernel(*refs):
    ins = (refs[0:6], refs[6:12])
    bm_ref, outs, s_ref = refs[12], refs[13:15], refs[15]
    j = pl.program_id(1)
    nchunk = TM // GD_CHUNK

    @pl.when(j == 0)
    def _init():
        s_ref[...] = jnp.zeros_like(s_ref)

    lane = lax.broadcasted_iota(I32, (1, 256), 1)
    s = [s_ref[0], s_ref[1]]
    dirs = range(2)
    for i in range(nchunk):
        cs = (i, nchunk - 1 - i)
        rows = [slice(c * GD_CHUNK, (c + 1) * GD_CHUNK) for c in cs]
        ld = lambda m: [ins[d][m][rows[d], :] for d in dirs]
        u, w, qk, qd, kd = ld(0), ld(1), ld(2), ld(3), ld(4)
        bds = [_block_diag(s[d].astype(BF16), bm_ref) for d in dirs]
        ws_qs = [_dot(jnp.concatenate([w[d], qd[d]], axis=0), bds[d]) for d in dirs]
        vb = [(u[d] - ws_qs[d][0:GD_CHUNK]).astype(BF16) for d in dirs]
        intra = [_dot(qk[d], _block_diag(vb[d], bm_ref)) for d in dirs]
        z = [_dot_tn(kd[d], vb[d]) for d in dirs]
        for d in dirs:
            outs[d][rows[d], :] = ws_qs[d][GD_CHUNK:] + intra[d]
            snew = s[d] * ins[d][5][cs[d] * SUBLANES:cs[d] * SUBLANES + 1, :]
            for h in range(GD_HEADS):
                snew = snew + jnp.where(lane // GD_HD == h, z[d][h * GD_HD:(h + 1) * GD_HD], 0.0)
            s[d] = snew
    s_ref[0] = s[0]
    s_ref[1] = s[1]


def _gdn_scan(u, w, qk, qd, kd, eg, bm256, nl_tiles):
    _, b, ntok, _ = u.shape
    nt = ntok // TM
    nc_tiles = nt - nl_tiles
    nchunk = TM // GD_CHUNK

    def tile_index(d, j):
        is_ctx = j < nc_tiles
        if d == 0:
            return jnp.where(is_ctx, nl_tiles + j, j - nc_tiles)
        return jnp.where(is_ctx, nl_tiles + nc_tiles - 1 - j, nl_tiles - 1 - (j - nc_tiles))

    tile = lambda d: pl.BlockSpec((None, None, TM, 256), lambda i, j: (d, i, tile_index(d, j), 0))
    egs = lambda d: pl.BlockSpec((None, None, nchunk * SUBLANES, 256), lambda i, j: (d, i, tile_index(d, j), 0))
    otile = lambda d: pl.BlockSpec((None, TM, 256), lambda i, j: (i, tile_index(d, j), 0))
    per_dir = lambda d: [tile(d)] * 5 + [egs(d)]
    return pl.pallas_call(
        _gdn_scan_kernel,
        grid=(b, nt),
        in_specs=per_dir(0) + per_dir(1) + [pl.BlockSpec(bm256.shape, lambda i, j: (0, 0))],
        out_specs=[otile(0), otile(1)],
        out_shape=[jax.ShapeDtypeStruct((b, ntok, 256), F32)] * 2,
        scratch_shapes=[pltpu.VMEM((2, GD_HD, 256), F32)],
        compiler_params=_params(("parallel", "arbitrary")),
        name="gdn_scan",
    )(u, w, qk, qd, kd, eg, u, w, qk, qd, kd, eg, bm256)


def _merge_kernel(x_ref, mod_ref, g1_ref, g2_ref, wg_ref, bg_ref, oa_ref, ob_ref, om_ref, of_ref, ob2_ref, gate_ref,
                  gdog_ref, gg_ref, wbr_ref, wo_ref, wrh_ref, wrl_ref, br_ref,
                  xn_o, h2_o, idx_o, gates_o):
    x = x_ref[...]
    mod = mod_ref[...]
    dm = D_MODEL
    sh1, sc1, gt1 = mod[:, 0:dm], mod[:, dm:2 * dm], mod[:, 2 * dm:3 * dm]
    sh2, sc2 = mod[:, 3 * dm:4 * dm], mod[:, 4 * dm:5 * dm]
    hb = (_rms_rows(x, g1_ref[...]) * (1.0 + sc1) + sh1).astype(BF16)

    o = of_ref[...] + ob2_ref[...]
    ss = _group_sum(o * o, gg_ref)
    od = (o * lax.rsqrt(ss * (1.0 / GD_HD) + EPS) * gdog_ref[...] * _silu(gate_ref[...])).astype(BF16)

    acc = jnp.zeros((TM, dm), F32)
    for n, br in enumerate((oa_ref[...], ob_ref[...], om_ref[...], od)):
        gpre = _dot(hb, wg_ref[:, n * dm:(n + 1) * dm]) + bg_ref[:, n * dm:(n + 1) * dm]
        acc = acc + _sigmoid(gpre) * _dot(br, wbr_ref[n])
    y = _dot(acc.astype(BF16), wo_ref[...])
    xn = x + gt1 * y
    xn_o[...] = xn
    h2 = _rms_rows(xn, g2_ref[...]) * (1.0 + sc2) + sh2
    for s in range(ROW_TILES):
        h2_o[pl.ds(s, TM, stride=ROW_TILES), :] = h2[:, s * LANES:(s + 1) * LANES]

    hh, hl = _split2(h2)
    logits = _dot(hh, wrh_ref[...]) + _dot(hl, wrh_ref[...]) + _dot(hh, wrl_ref[...]) + br_ref[...]
    lane = lax.broadcasted_iota(I32, logits.shape, 1)
    vals = jnp.full(logits.shape, NEG_BIG, F32)
    idxs = jnp.zeros(logits.shape, I32)
    l = logits
    for k in range(TOP_K):
        m = jnp.max(l, axis=-1, keepdims=True)
        i = jnp.min(jnp.where(l == m, lane, LANES), axis=-1, keepdims=True)
        vals = jnp.where(lane == k, m, vals)
        idxs = jnp.where(lane == k, i, idxs)
        l = jnp.where(lane == i, NEG_BIG * 2.0, l)
    e = jnp.exp(vals - jnp.max(vals, axis=-1, keepdims=True))
    e = jnp.where(lane < TOP_K, e, 0.0)
    gates_o[...] = e / jnp.sum(e, axis=-1, keepdims=True)
    idx_o[...] = idxs


def _merge(x_all, modsel, g1, g2, wg, bg, oa, ob, om, o2, gate, gdog, gg64, wbr, wo, wrh, wrl, brp, n_tiles, nl_tiles):
    b = x_all.shape[0]
    ntok_out = n_tiles * TM
    tok = lambda w: pl.BlockSpec((None, TM, w), lambda i, t: (i, t, 0))
    full = lambda a: pl.BlockSpec(a.shape, lambda i, t: (0,) * a.ndim)
    return pl.pallas_call(
        _merge_kernel,
        grid=(b, n_tiles),
        in_specs=[tok(D_MODEL),
                  pl.BlockSpec((None, None, 1, 6 * D_MODEL), lambda i, t: (i, t // nl_tiles, 0, 0)),
                  full(g1), full(g2), full(wg), full(bg), tok(256), tok(256), tok(256), tok(256), tok(256), tok(256),
                  full(gdog), full(gg64), full(wbr), full(wo), full(wrh), full(wrl), full(brp)],
        out_specs=[tok(D_MODEL), pl.BlockSpec((None, TM * ROW_TILES, LANES), lambda i, t: (i, t, 0)),
                   tok(LANES), tok(LANES)],
        out_shape=[jax.ShapeDtypeStruct((b, ntok_out, D_MODEL), F32),
                   jax.ShapeDtypeStruct((b, ntok_out * ROW_TILES, LANES), F32),
                   jax.ShapeDtypeStruct((b, ntok_out, LANES), I32), jax.ShapeDtypeStruct((b, ntok_out, LANES), F32)],
        compiler_params=_params(("parallel", "parallel")),
        name="merge",
    )(x_all, modsel, g1, g2, wg, bg, oa, ob, om, o2[0], o2[1], gate, gdog, gg64, wbr, wo, wrh, wrl, brp)


def _expert_kernel(bexp_ref, src0_ref, src_ref, dst_ref, tok_hbm, wup_ref, bup_ref, wdn_ref, bdn_ref, out_hbm,
                   x0, x1, x2, x3, y0, y1, y2, y3, gsem, ssem):
    s = pl.program_id(0)
    ns = pl.num_programs(0)
    xs = (x0, x1, x2, x3)
    ys = (y0, y1, y2, y3)

    def tile_rows(first):
        return pl.ds(pl.multiple_of(first, ROW_TILES), ROW_TILES)

    def issue_gather(idx_ref, blk, k):
        for r in range(EXP_SUB):
            pltpu.make_async_copy(tok_hbm.at[tile_rows(idx_ref[blk, r]), :], xs[k].at[pl.ds(r * ROW_TILES, ROW_TILES), :],
                                  gsem.at[k]).start(priority=r % DMA_PRIORITIES)

    def wait_gather(k):
        pltpu.make_async_copy(tok_hbm.at[pl.ds(0, EXP_SUB * ROW_TILES), :], xs[k], gsem.at[k]).wait()

    def issue_scatter(k):
        for r in range(EXP_SUB):
            pltpu.make_async_copy(ys[k].at[pl.ds(r * ROW_TILES, ROW_TILES), :], out_hbm.at[tile_rows(dst_ref[k, r]), :],
                                  ssem.at[k]).start(priority=r % DMA_PRIORITIES)

    def wait_scatter(k):
        pltpu.make_async_copy(ys[k], out_hbm.at[pl.ds(0, EXP_SUB * ROW_TILES), :], ssem.at[k]).wait()

    @pl.when(s == 0)
    def _prologue():
        issue_gather(src0_ref, 0, 0)
        issue_gather(src0_ref, 1, 1)

    for k in range(EXP_NSUB):
        wait_gather(k)

        @pl.when(s > 0)
        def _free_y():
            wait_scatter(k)

        if k >= 1:
            issue_scatter(k - 1)
        issue_gather(src_ref, k, (k + 2) % EXP_NSUB)
        xb = jnp.concatenate([xs[k][pl.ds(t, EXP_SUB, stride=ROW_TILES), :].astype(BF16) for t in range(ROW_TILES)],
                             axis=1)
        z = _dot(xb, wup_ref[...]) + bup_ref[...]
        glu = jnp.minimum(z[:, :D_FF], SWIGLU_LIMIT)
        lin = jnp.clip(z[:, D_FF:], -SWIGLU_LIMIT, SWIGLU_LIMIT)
        act = glu * _sigmoid(SWIGLU_ALPHA * glu) * (lin + 1.0)
        y = _dot(act.astype(BF16), wdn_ref[...]) + bdn_ref[...]
        for t in range(ROW_TILES):
            ys[k][pl.ds(t, EXP_SUB, stride=ROW_TILES), :] = y[:, t * LANES:(t + 1) * LANES]
    issue_scatter(EXP_NSUB - 1)

    @pl.when(s == ns - 1)
    def _epilogue():
        for k in range(EXP_NSUB):
            wait_scatter(k)
        wait_gather(0)
        wait_gather(1)


def _experts(blk_exp, row_src, row_dst, tok, wup, bup, wdn, bdn):
    ns = blk_exp.shape[0]
    n_rows = ns * EXP_TILE
    src = jnp.concatenate([row_src, jnp.zeros((2 * EXP_SUB,), I32)])
    src0 = src[:2 * EXP_SUB].reshape(2, EXP_SUB)
    src_ahead = jnp.concatenate([src[2 * EXP_SUB:], jnp.zeros((2 * EXP_SUB,), I32)])[:n_rows]
    idx = lambda: pl.BlockSpec((None, EXP_NSUB, EXP_SUB), lambda i, be: (i, 0, 0), memory_space=pltpu.SMEM)
    gs = pltpu.PrefetchScalarGridSpec(
        num_scalar_prefetch=1,
        grid=(ns,),
        in_specs=[pl.BlockSpec(memory_space=pltpu.SMEM),
                  idx(), idx(),
                  pl.BlockSpec(memory_space=pl.ANY),
                  pl.BlockSpec((None, D_MODEL, 2 * D_FF), lambda i, be: (be[i], 0, 0)),
                  pl.BlockSpec((None, 1, 2 * D_FF), lambda i, be: (be[i], 0, 0)),
                  pl.BlockSpec((None, D_FF, D_MODEL), lambda i, be: (be[i], 0, 0)),
                  pl.BlockSpec((None, 1, D_MODEL), lambda i, be: (be[i], 0, 0))],
        out_specs=pl.BlockSpec(memory_space=pl.ANY),
        scratch_shapes=[pltpu.VMEM((EXP_SUB * ROW_TILES, LANES), F32)] * (2 * EXP_NSUB) +
                       [pltpu.SemaphoreType.DMA((EXP_NSUB,)), pltpu.SemaphoreType.DMA((EXP_NSUB,))],
    )
    return pl.pallas_call(
        _expert_kernel,
        grid_spec=gs,
        out_shape=jax.ShapeDtypeStruct((n_rows * ROW_TILES, LANES), F32),
        compiler_params=_params(("arbitrary",)),
        name="experts",
    )(blk_exp, src0, src_ahead.reshape(ns, EXP_NSUB, EXP_SUB), row_dst.reshape(ns, EXP_NSUB, EXP_SUB),
      tok, wup, bup, wdn, bdn)


def _combine_kernel(x_ref, mod_ref, gates_ref, y0_ref, y1_ref, y2_ref, y3_ref, o_ref):
    g2 = mod_ref[...][:, 5 * D_MODEL:6 * D_MODEL]
    gates = gates_ref[...]
    for s in range(ROW_TILES):
        cols = slice(s * LANES, (s + 1) * LANES)
        f = jnp.zeros((TM, LANES), F32)
        for k, y_ref in enumerate((y0_ref, y1_ref, y2_ref, y3_ref)):
            f = f + gates[:, k:k + 1] * y_ref[pl.ds(s, TM, stride=ROW_TILES), :]
        o_ref[:, cols] = x_ref[:, cols] + g2[:, cols] * f


def _combine(xn, modsel, gates, out4, nl_tiles):
    b, ntok, _ = xn.shape
    nt = ntok // TM
    slab = b * nt
    tok = lambda w: pl.BlockSpec((None, TM, w), lambda i, t: (i, t, 0))
    ys = lambda k: pl.BlockSpec((TM * ROW_TILES, LANES), lambda i, t: (k * slab + i * nt + t, 0))
    return pl.pallas_call(
        _combine_kernel,
        grid=(b, nt),
        in_specs=[tok(D_MODEL),
                  pl.BlockSpec((None, None, 1, 6 * D_MODEL), lambda i, t: (i, t // nl_tiles, 0, 0)),
                  tok(LANES), ys(0), ys(1), ys(2), ys(3)],
        out_specs=tok(D_MODEL),
        out_shape=jax.ShapeDtypeStruct((b, ntok, D_MODEL), F32),
        compiler_params=_params(("parallel", "parallel")),
        name="combine",
    )(xn, modsel, gates, out4, out4, out4, out4)


def _np_consts():
    lane = np.arange(256)
    half, r = lane // 128, lane % 128
    s, j = r // 16, r % 16
    d = np.where(j < 8, j + 8 * half, 16 + (j - 8) + 8 * half)
    perm_da = s * 32 + d
    dperm_da = d
    h, j = r // 32, r % 32
    d = np.where(j < 16, j + 16 * half, 32 + (j - 16) + 16 * half)
    perm_gq = h * 64 + d
    perm_gk = (h // 2) * 64 + d
    dperm_ga = d
    hv, dv = lane // 64, lane % 64
    perm_gv = (hv // 2) * 64 + dv
    grp = lambda key: (key[:, None] == key[None, :]).astype(np.float32)
    gda = grp((lane % 128) // 16)
    gga = grp((lane % 128) // 32)
    g64 = grp(lane // 64)
    stack2 = lambda g: np.concatenate([g, g], axis=0)
    bm256 = (np.arange(256)[:, None] // 64 == lane[None, :] // 64).astype(np.float32)
    bm512 = (np.arange(512)[:, None] // 128 == lane[None, :] // 64).astype(np.float32)
    i = np.arange(64)
    jj = lane % 64
    lmat, lmat2, pmask = [], [], []
    for dirn in range(2):
        if dirn == 0:
            inc = (i[None, :] <= i[:, None]); aft = (i[None, :] > i[:, None])
            p_incl = (i[:, None] >= jj[None, :]); p_strict = (i[:, None] > jj[None, :]); p_after = (i[:, None] > jj[None, :])
        else:
            inc = (i[None, :] >= i[:, None]); aft = (i[None, :] < i[:, None])
            p_incl = (i[:, None] <= jj[None, :]); p_strict = (i[:, None] < jj[None, :]); p_after = (i[:, None] < jj[None, :])
        ones = np.ones((64, 64))
        l1 = np.concatenate([inc, aft, ones], axis=0).astype(np.float32)
        lmat.append(np.concatenate([l1, l1, l1], axis=1))
        l2 = inc.astype(np.float32)
        lmat2.append(np.concatenate([l2, l2, l2], axis=1))
        eye = (i[:, None] == jj[None, :])
        pmask.append(np.stack([p_incl, p_strict, p_after, eye]).astype(np.float32))
    bi, bj = i[:, None], jj[None, :]
    imask = np.stack([bi // 8 == bj // 8,
                      (bi // 16 == bj // 16) & (bi // 8 != bj // 8),
                      (bi // 32 == bj // 32) & (bi // 16 != bj // 16),
                      bi // 32 != bj // 32]).astype(np.float32)
    eg = np.zeros((2, 128, 256), np.float32)
    eb = np.zeros((2, 128, 256), np.float32)
    for dirn in range(2):
        for hh in range(GD_HEADS):
            eg[dirn, dirn * GD_HEADS + hh, hh * 64:(hh + 1) * 64] = 1.0
            eb[dirn, 2 * GD_HEADS + dirn * GD_HEADS + hh, hh * 64:(hh + 1) * 64] = 1.0
    eg = np.concatenate([eg, eg, eg], axis=1)
    eb = np.concatenate([eb, eb, eb], axis=1)
    return dict(perm_da=perm_da, dperm_da=dperm_da, perm_gq=perm_gq, perm_gk=perm_gk, dperm_ga=dperm_ga, perm_gv=perm_gv,
                gda=stack2(gda), gga=stack2(gga), g64=stack2(g64), bm256=bm256, bm512=bm512,
                lmat=np.stack(lmat), lmat2=np.stack(lmat2), pmask=np.stack(pmask), imask=imask, eg=eg, eb=eb)


def _rope_tables(seq, ctx_len, quarter, reps):
    t = jnp.arange(seq)
    row = (t // GRID_W).astype(F32)
    col = (t % GRID_W).astype(F32)
    inv = ROPE_THETA ** (-jnp.arange(quarter, dtype=F32) / quarter)
    ang = jnp.concatenate([row[:, None] * inv, col[:, None] * inv], axis=1)
    ang = jnp.tile(ang, (1, reps))
    cos = jnp.concatenate([jnp.cos(ang), jnp.ones((ctx_len, LANES), F32)], axis=0)
    sin = jnp.concatenate([jnp.sin(ang), jnp.zeros((ctx_len, LANES), F32)], axis=0)
    return cos, sin


def _routing(idx, n_tok):
    n_asg = n_tok * TOP_K
    e_flat = idx.reshape(-1)
    order = jnp.argsort(e_flat, stable=True).astype(I32)
    counts = jnp.sum((e_flat[:, None] == jnp.arange(N_EXPERTS, dtype=I32)[None, :]).astype(I32), axis=0)
    padded = (counts + EXP_TILE - 1) // EXP_TILE * EXP_TILE
    start = jnp.cumsum(counts) - counts
    pend = jnp.cumsum(padded)
    pstart = pend - padded
    n_tiles = (n_asg + N_EXPERTS * (EXP_TILE - 1)) // EXP_TILE + 1
    n_rows = n_tiles * EXP_TILE
    tile_row0 = jnp.arange(n_tiles, dtype=I32) * EXP_TILE
    tile_exp = jnp.minimum(jnp.sum((pend[None, :] <= tile_row0[:, None]).astype(I32), axis=1), N_EXPERTS - 1)
    row = jnp.arange(n_rows, dtype=I32)
    e_row = jnp.repeat(tile_exp, EXP_TILE)
    local = row - pstart[e_row]
    valid = local < counts[e_row]
    asg = order[jnp.minimum(start[e_row] + local, n_asg - 1)]
    tok_of = asg // TOP_K
    row_src = jnp.where(valid, tok_of, 0)
    n_pad_before = row - (start[e_row] + counts[e_row])
    row_dst = jnp.where(valid, (asg % TOP_K) * n_tok + tok_of, n_asg + n_pad_before)
    return tile_exp.astype(I32), (row_src * ROW_TILES).astype(I32), (row_dst * ROW_TILES).astype(I32)


def kernel(x, c, ctx, c_ctx, w_mod, b_mod, norm1_g, norm2_g, w_in, b_gate, da_q_g, da_k_g, da_lam_q1, da_lam_k1,
           da_lam_q2, da_lam_k2, da_sub_g, ga_q_g, ga_k_g, gm_v_g, gm_ws, gm_bs, gd_conv_w, gd_a_log, gd_dt_bias,
           gd_out_g, w_br, w_o, w_router, b_router, w_up, b_up, w_down, b_down):
    bsz, seq, dm = x.shape
    ctx_len = ctx.shape[1]
    depth = w_mod.shape[0]
    assert dm == D_MODEL and seq % TM == 0 and ctx_len % TM == 0 and seq % ctx_len == 0 and seq % GRID_W == 0
    ntok = seq + ctx_len
    nl_tiles = seq // TM
    nt_tiles = ntok // TM
    cn = _np_consts()
    bfc = lambda a: jnp.asarray(a, BF16)
    gda, gga, g64 = bfc(cn['gda']), bfc(cn['gga']), bfc(cn['g64'])
    bm256, bm512 = bfc(cn['bm256']), bfc(cn['bm512'])
    lmat, lmat2 = bfc(cn['lmat']), bfc(cn['lmat2'])
    pmask = jnp.asarray(cn['pmask'], F32)
    imask = jnp.asarray(cn['imask'], F32)
    eg, eb = bfc(cn['eg']), bfc(cn['eb'])
    cosa, sina = _rope_tables(seq, ctx_len, DA_HD // 4, LANES // (DA_HD // 2))
    cosg, sing = _rope_tables(seq, ctx_len, GA_HD // 4, LANES // (GA_HD // 2))
    cuts = np.cumsum((256, 256, 256, 256, 128, 128, 256, 256, 768, 256, 8, 8))
    mrows = -(-(bsz + 1) // SUBLANES) * SUBLANES
    cpad = jnp.zeros((mrows, dm), F32).at[:bsz].set(c).at[bsz].set(c_ctx)

    x_all = jnp.concatenate([x, ctx], axis=1)
    for l in range(depth):
        need_ctx = l < depth - 1
        lam_init = 0.8 - 0.6 * math.exp(-0.3 * l)
        mod = _adaln(cpad, w_mod[l], b_mod[l])
        modsel = jnp.stack([mod[:bsz], jnp.broadcast_to(mod[bsz], (bsz, 6 * dm))], axis=1).reshape(bsz, 2, 1, 6 * dm)

        wl = w_in[l]
        aq, ak, av, gq, gk, gv, mu, mv, dqkv, dgate, dda, ddb = jnp.split(wl[:, :cuts[-1]], cuts[:-1].tolist(), axis=1)
        wcat = jnp.concatenate([aq[:, cn['perm_da']], ak[:, cn['perm_da']], av, gq[:, cn['perm_gq']],
                                gk[:, cn['perm_gk']], gv[:, cn['perm_gv']], mu, mv, dqkv, dgate, dda, ddb,
                                jnp.zeros((dm, LANES - 4 * GD_HEADS), F32)], axis=1).astype(BF16)
        row = lambda a: a.reshape(1, -1).astype(F32)
        daqg = row(da_q_g[l][cn['dperm_da']])
        dakg = row(da_k_g[l][cn['dperm_da']])
        gaqg = row(ga_q_g[l][cn['dperm_ga']])
        gakg = row(ga_k_g[l][cn['dperm_ga']])
        wsp = gm_ws[l].transpose(1, 0, 2).reshape(GM_CHUNK, GM_GROUPS * GM_CHUNK).astype(BF16)
        bsp = jnp.repeat(gm_bs[l].T, GM_GW, axis=1)
        consts = (gda, gga, cosa, sina, cosg, sing, daqg, dakg, gaqg, gakg, row(gm_v_g[l]), wsp, bsp, bm512)
        daq, dak, dav, gaq, gak, gav, om, gdqkv, gdgate, gdab = _inproj(
            x_all, modsel, row(norm1_g[l]), wcat, consts, nl_tiles)

        lam = (jnp.exp(jnp.sum(da_lam_q1[l] * da_lam_k1[l])) - jnp.exp(jnp.sum(da_lam_q2[l] * da_lam_k2[l]))
               + lam_init).reshape(1).astype(F32)
        subg = row(jnp.tile(da_sub_g[l], DA_HEADS))
        att = functools.partial(_attention, lam, gg64=g64, subg=subg)
        lat = dict(q_start=0, q_len=seq, kv_start=0, kv_len=ntok)
        oa = att(daq, dak, dav, diff=True, post_scale=1.0 - lam_init, **lat)
        ob = att(gaq, gak, gav, diff=False, post_scale=1.0, **lat)
        if need_ctx:
            cx = dict(q_start=seq, q_len=ctx_len, kv_start=seq, kv_len=ctx_len)
            oa = jnp.concatenate([oa, att(daq, dak, dav, diff=True, post_scale=1.0 - lam_init, **cx)], axis=1)
            ob = jnp.concatenate([ob, att(gaq, gak, gav, diff=False, post_scale=1.0, **cx)], axis=1)

        nea = jnp.zeros((1, LANES), F32).at[0, :2 * GD_HEADS].set(-jnp.exp(gd_a_log[l].reshape(-1)))
        dtb = jnp.zeros((1, LANES), F32).at[0, :2 * GD_HEADS].set(gd_dt_bias[l].reshape(-1))
        gconsts = (gd_conv_w[l], nea, dtb, eg, eb, lmat, lmat2, pmask, imask, g64, bm256)
        u, w, qk, qd, kd, egl = _gdn_local(gdqkv, gdab, gconsts, nl_tiles)
        o2 = _gdn_scan(u, w, qk, qd, kd, egl, bm256, nl_tiles)

        n_tiles = nt_tiles if need_ctx else nl_tiles
        wg = wl[:, cuts[-1]:].astype(BF16)
        wr = jnp.zeros((dm, LANES), F32).at[:, :N_EXPERTS].set(w_router[l])
        wrh = wr.astype(BF16)
        wrl = (wr - wrh.astype(F32)).astype(BF16)
        brp = jnp.full((1, LANES), NEG_BIG, F32).at[0, :N_EXPERTS].set(b_router[l])
        xn, h2, idx, gates = _merge(x_all, modsel, row(norm1_g[l]), row(norm2_g[l]), wg, row(b_gate[l]),
                                    oa, ob, om, o2, gdgate, row(jnp.tile(gd_out_g[l], GD_HEADS)), g64,
                                    w_br[l].astype(BF16), w_o[l].astype(BF16), wrh, wrl, brp, n_tiles, nl_tiles)

        n_tok = bsz * n_tiles * TM
        tile_exp, row_src, row_dst = _routing(idx[:, :, :TOP_K], n_tok)
        out4 = _experts(tile_exp, row_src, row_dst, h2.reshape(n_tok * ROW_TILES, LANES), w_up[l].astype(BF16),
                        b_up[l].reshape(N_EXPERTS, 1, 2 * D_FF), w_down[l].astype(BF16),
                        b_down[l].reshape(N_EXPERTS, 1, dm))
        x_all = _combine(xn, modsel, gates, out4, nl_tiles)
    return x_all[:, :seq] if x_all.shape[1] != seq else x_all
```

```python
import functools
import math

import numpy as np
import jax
import jax.numpy as jnp
from jax import lax
from jax.experimental import pallas as pl
from jax.experimental.pallas import tpu as pltpu

F32 = jnp.float32
BF16 = jnp.bfloat16
I32 = jnp.int32

D_MODEL = 1024
GRID_W = 64
EPS = 1e-6
ROPE_THETA = 10000.0
DA_HEADS, DA_HD = 4, 32
GA_HEADS, GA_KV, GA_HD = 4, 2, 64
GM_GROUPS, GM_GW, GM_CHUNK = 4, 64, 128
GD_HEADS, GD_HD, GD_CONV, GD_CHUNK = 4, 64, 5, 64
N_BRANCH, BR_W = 4, 256
N_EXPERTS, TOP_K, D_FF = 32, 4, 1024
SWIGLU_ALPHA, SWIGLU_LIMIT = 1.702, 7.0
EXP_SUB = 256
EXP_NSUB = 4
EXP_TILE = EXP_SUB * EXP_NSUB
DMA_PRIORITIES = 2

LANES = 128
SUBLANES = 8
TM = 256
TQ = 128
TQ_PER_STEP = 2
TQ_PER_STEP_GQA = 4
HALO = SUBLANES
ROW_TILES = D_MODEL // LANES
VMEM_LIMIT = 56 * 1024 * 1024
NEG_BIG = -1e30
LOG2E = math.log2(math.e)

C_DAQ, C_DAK, C_DAV, C_GAQ, C_GAK, C_GAV, C_GMU, C_GMV, C_GDQKV, C_GDGATE, C_GDAB, C_END = (
    0, 256, 512, 768, 1024, 1280, 1536, 1792, 2048, 2816, 3072, 3200)


def _params(sem):
    return pltpu.CompilerParams(dimension_semantics=sem, vmem_limit_bytes=VMEM_LIMIT)


def _split2(x):
    hi = x.astype(BF16)
    lo = (x - hi.astype(F32)).astype(BF16)
    return hi, lo


def _split3(x):
    hi = x.astype(BF16)
    r = x - hi.astype(F32)
    mid = r.astype(BF16)
    lo = (r - mid.astype(F32)).astype(BF16)
    return hi, mid, lo


def _dot(a, b):
    return jnp.dot(a, b, preferred_element_type=F32)


def _dot_nt(a, b):
    return lax.dot_general(a, b, (((1,), (1,)), ((), ())), preferred_element_type=F32)


def _dot_tn(a, b):
    return lax.dot_general(a, b, (((0,), (0,)), ((), ())), preferred_element_type=F32)


def _group_sum(x, gg_ref):
    hi, lo = _split2(x)
    return _dot(jnp.concatenate([hi, lo], axis=1), gg_ref[...])


def _sigmoid(x):
    return 1.0 / (1.0 + jnp.exp(-x))


def _silu(x):
    return x * _sigmoid(x)


def _gelu(x):
    return 0.5 * x * (1.0 + lax.erf(x * (1.0 / math.sqrt(2.0))))


def _softplus(x):
    return jnp.maximum(x, 0.0) + jnp.log1p(jnp.exp(-jnp.abs(x)))


def _rms_rows(x, gain):
    return x * lax.rsqrt(jnp.mean(x * x, axis=-1, keepdims=True) + EPS) * gain


def _block_diag(y, mask_ref):
    return jnp.concatenate([y, y, y, y], axis=0) * mask_ref[...]


def _adaln_kernel(c_ref, w_ref, b_ref, o_ref):
    a = _silu(c_ref[...])
    ah, al = _split2(a)
    w = w_ref[...]
    wh, wl = _split2(w)
    o_ref[...] = _dot(ah, wh) + _dot(al, wh) + _dot(ah, wl) + b_ref[...]


def _adaln(cpad, w_mod, b_mod):
    m = cpad.shape[0]
    n = w_mod.shape[1]
    tn = 512
    return pl.pallas_call(
        _adaln_kernel,
        grid=(n // tn,),
        in_specs=[pl.BlockSpec((m, D_MODEL), lambda j: (0, 0)),
                  pl.BlockSpec((D_MODEL, tn), lambda j: (0, j)),
                  pl.BlockSpec((1, tn), lambda j: (0, j))],
        out_specs=pl.BlockSpec((m, tn), lambda j: (0, j)),
        out_shape=jax.ShapeDtypeStruct((m, n), F32),
        compiler_params=_params(("arbitrary",)),
        name="adaln",
    )(cpad, w_mod, b_mod.reshape(1, n))


def _qk_prep(z, gg_ref, nd, gain_ref, cos_ref, sin_ref, scale):
    ss = _group_sum(z * z, gg_ref)
    y = z * lax.rsqrt(ss * (1.0 / nd) + EPS) * gain_ref[...]
    a = y[:, :LANES]
    b = y[:, LANES:]
    c = cos_ref[...]
    s = sin_ref[...]
    out = jnp.concatenate([a * c - b * s, b * c + a * s], axis=1)
    if scale != 1.0:
        out = out * scale
    return out


def _inproj_kernel(x_ref, mod_ref, g1_ref, w_ref, gda_ref, gga_ref, cosa_ref, sina_ref, cosg_ref, sing_ref,
                   daqg_ref, dakg_ref, gaqg_ref, gakg_ref, gmvg_ref, wsp_ref, bsp_ref, bm_ref,
                   daq_o, dak_o, dav_o, gaq_o, gak_o, gav_o, om_o, gdqkv_o, gdgate_o, gdab_o):
    x = x_ref[...]
    mod = mod_ref[...]
    sh = mod[:, 0:D_MODEL]
    sc = mod[:, D_MODEL:2 * D_MODEL]
    hb = (_rms_rows(x, g1_ref[...]) * (1.0 + sc) + sh).astype(BF16)

    def proj(lo, hi):
        return _dot(hb, w_ref[:, lo:hi])

    daq_o[...] = _qk_prep(proj(C_DAQ, C_DAK), gda_ref, DA_HD, daqg_ref, cosa_ref, sina_ref, LOG2E * DA_HD ** -0.5).astype(BF16)
    dak_o[...] = _qk_prep(proj(C_DAK, C_DAV), gda_ref, DA_HD, dakg_ref, cosa_ref, sina_ref, 1.0).astype(BF16)
    dav_o[...] = proj(C_DAV, C_GAQ).astype(BF16)
    gaq_o[...] = _qk_prep(proj(C_GAQ, C_GAK), gga_ref, GA_HD, gaqg_ref, cosg_ref, sing_ref, LOG2E * GA_HD ** -0.5).astype(BF16)
    gak_o[...] = _qk_prep(proj(C_GAK, C_GAV), gga_ref, GA_HD, gakg_ref, cosg_ref, sing_ref, 1.0).astype(BF16)
    gav_o[...] = proj(C_GAV, C_GMU).astype(BF16)

    u = _gelu(proj(C_GMU, C_GMV))
    v = _rms_rows(_gelu(proj(C_GMV, C_GDQKV)), gmvg_ref[...])
    for j in range(TM // GM_CHUNK):
        rows = slice(j * GM_CHUNK, (j + 1) * GM_CHUNK)
        bd = _block_diag(v[rows].astype(BF16), bm_ref)
        mixed = _dot(wsp_ref[...], bd) + bsp_ref[...]
        om_o[rows, :] = (u[rows] * mixed).astype(BF16)

    gdqkv_o[...] = proj(C_GDQKV, C_GDGATE)
    gdgate_o[...] = proj(C_GDGATE, C_GDAB)
    gdab_o[...] = proj(C_GDAB, C_END)


def _inproj(x_all, modsel, g1, wcat, consts, nl_tiles):
    b, ntok, _ = x_all.shape
    nt = ntok // TM
    tok = lambda w: pl.BlockSpec((None, TM, w), lambda i, t: (i, t, 0))
    full = lambda a: pl.BlockSpec(a.shape, lambda i, t: (0,) * a.ndim)
    tab = pl.BlockSpec((TM, LANES), lambda i, t: (t, 0))
    (gda, gga, cosa, sina, cosg, sing, daqg, dakg, gaqg, gakg, gmvg, wsp, bsp, bm512) = consts
    in_specs = [tok(D_MODEL),
                pl.BlockSpec((None, None, 1, 6 * D_MODEL), lambda i, t: (i, t // nl_tiles, 0, 0)),
                full(g1), full(wcat), full(gda), full(gga), tab, tab, tab, tab,
                full(daqg), full(dakg), full(gaqg), full(gakg), full(gmvg), full(wsp), full(bsp), full(bm512)]
    widths = [(256, BF16)] * 7 + [(768, F32), (256, F32), (128, F32)]
    return pl.pallas_call(
        _inproj_kernel,
        grid=(b, nt),
        in_specs=in_specs,
        out_specs=[tok(w) for w, _ in widths],
        out_shape=[jax.ShapeDtypeStruct((b, ntok, w), dt) for w, dt in widths],
        compiler_params=_params(("parallel", "parallel")),
        name="inproj",
    )(x_all, modsel, g1, wcat, gda, gga, cosa, sina, cosg, sing, daqg, dakg, gaqg, gakg, gmvg, wsp, bsp, bm512)


def _attn_kernel(lam_ref, q_ref, k_ref, v_ref, gg_ref, subg_ref, o_ref, *, diff, qlanes, post_scale):
    k = k_ref[...]
    v = v_ref[...]
    tq = TQ
    lane = lax.broadcasted_iota(I32, (1, 2 * LANES), 1)
    qhead = (lane % LANES) // qlanes
    vhead = lane // 64
    lam = lam_ref[0]
    nloop = DA_HEADS if diff else GA_HEADS // 2
    units = [(qt, h) for qt in range(q_ref.shape[0] // TQ) for h in range(nloop)]

    def scores(unit):
        qt, h = unit
        q = q_ref[qt * TQ:(qt + 1) * TQ, :]
        zero = jnp.zeros_like(q)
        lhs = jnp.concatenate([jnp.where(qhead == 2 * h, q, zero), jnp.where(qhead == 2 * h + 1, q, zero)], axis=0)
        return _dot_nt(lhs, k)

    def softmax(s):
        m = jnp.max(s, axis=-1, keepdims=True)
        p = jnp.exp2(s - m)
        l = jnp.sum(p, axis=-1, keepdims=True)
        return p, l

    accs = [jnp.zeros((tq, 2 * LANES), F32) for _ in range(q_ref.shape[0] // TQ)]
    s_next = scores(units[0])
    for n, (qt, h) in enumerate(units):
        s_cur = s_next
        if n + 1 < len(units):
            s_next = scores(units[n + 1])
        p, l = softmax(s_cur)
        if diff:
            a = (p[:tq] - p[tq:] * (lam * l[:tq] / l[tq:])).astype(BF16)
            accs[qt] = accs[qt] + jnp.where(vhead == h, _dot(a, v) / l[:tq], 0.0)
        else:
            o = _dot(p.astype(BF16), v) / l
            accs[qt] = accs[qt] + jnp.where(vhead == 2 * h, o[:tq], 0.0) + jnp.where(vhead == 2 * h + 1, o[tq:], 0.0)
    for qt, acc in enumerate(accs):
        if diff:
            ss = _group_sum(acc * acc, gg_ref)
            acc = acc * lax.rsqrt(ss * (1.0 / (2 * DA_HD)) + EPS) * subg_ref[...] * post_scale
        o_ref[qt * TQ:(qt + 1) * TQ, :] = acc.astype(BF16)


def _attention(lam, q, k, v, gg64, subg, *, diff, post_scale, q_start, q_len, kv_start, kv_len):
    b = q.shape[0]
    tqb = TQ * min(TQ_PER_STEP if diff else TQ_PER_STEP_GQA, q_len // TQ)
    qoff = q_start // tqb
    kvoff = kv_start // kv_len
    kern = functools.partial(_attn_kernel, diff=diff, qlanes=16 if diff else 32, post_scale=post_scale)
    full = lambda a: pl.BlockSpec(a.shape, lambda i, j: (0,) * a.ndim)
    kvspec = pl.BlockSpec((None, kv_len, 256), lambda i, j: (i, kvoff, 0))
    return pl.pallas_call(
        kern,
        grid=(b, q_len // tqb),
        in_specs=[pl.BlockSpec(memory_space=pltpu.SMEM),
                  pl.BlockSpec((None, tqb, 256), lambda i, j: (i, qoff + j, 0)),
                  kvspec, kvspec, full(gg64), full(subg)],
        out_specs=pl.BlockSpec((None, tqb, 256), lambda i, j: (i, j, 0)),
        out_shape=jax.ShapeDtypeStruct((b, q_len, 256), BF16),
        compiler_params=_params(("parallel", "parallel")),
        name="diff_attn" if diff else "gqa_attn",
    )(lam, q, k, v, gg64, subg)


def _gdn_local_kernel(xm_ref, xl_ref, xr_ref, ab_ref, cw_ref, nea_ref, dtb_ref, eg_ref, eb_ref,
                      lmat_ref, lmat2_ref, pmask_ref, imask_ref, gg_ref, bm_ref,
                      u_o, w_o, qk_o, qd_o, kd_o, eg_o, xext, *, nl_tiles, nt_tiles):
    t = pl.program_id(1)
    nchunk = TM // GD_CHUNK

    left_ok = jnp.logical_and(t != 0, t != nl_tiles)
    right_ok = jnp.logical_and(t != nl_tiles - 1, t != nt_tiles - 1)
    xext[pl.ds(0, HALO), :] = jnp.where(left_ok, xl_ref[...], 0.0)
    xext[pl.ds(HALO, TM), :] = xm_ref[...]
    xext[pl.ds(HALO + TM, HALO), :] = jnp.where(right_ok, xr_ref[...], 0.0)
    y = jnp.zeros((TM, 3 * 256), F32)
    for j in range(GD_CONV):
        y = y + xext[pl.ds(HALO - GD_CONV // 2 + j, TM), :] * cw_ref[pl.ds(j, 1), :]
    y = _silu(y)
    qv = y[:, 0:256]
    kv = y[:, 256:512]
    q = qv * lax.rsqrt(_group_sum(qv * qv, gg_ref) + EPS) * (GD_HD ** -0.5)
    k = kv * lax.rsqrt(_group_sum(kv * kv, gg_ref) + EPS)
    v = y[:, 512:768]
    ab = ab_ref[...]
    lane = lax.broadcasted_iota(I32, (1, LANES), 1)
    gval = nea_ref[...] * _softplus(ab + dtb_ref[...])
    g1, g2, g3 = _split3(jnp.where(lane < 2 * GD_HEADS, gval, _sigmoid(ab)))
    src3 = jnp.concatenate([g1, g2, g3], axis=1)

    def mm3(xp, yp):
        xh, xl = _split2(xp)
        yh, yl = _split2(yp)
        bdh = _block_diag(yh, bm_ref)
        lhs = jnp.concatenate([xh, xl, xh], axis=1)
        rhs = jnp.concatenate([bdh, bdh, _block_diag(yl, bm_ref)], axis=0)
        return _dot(lhs, rhs)

    chains = [(d, c) for d in range(2) for c in range(nchunk)]
    each = lambda f, *lists: [f(*a) for a in zip(*lists)]
    rows = [slice(c * GD_CHUNK, (c + 1) * GD_CHUNK) for _, c in chains]
    gexp = [_dot(src3, eg_ref[d]) for d in range(2)]
    bexp = [_dot(src3, eb_ref[d]) for d in range(2)]
    qc = [q[r] for r in rows]
    kc = [k[r] for r in rows]
    vc = [v[r] for r in rows]
    gch = [gexp[d][r] for (d, _), r in zip(chains, rows)]
    bx = [bexp[d][r] for (d, _), r in zip(chains, rows)]
    mask = lambda m: [pmask_ref[d, m] for d, _ in chains]
    cums = [_dot(lmat_ref[d], jnp.concatenate(_split3(g), axis=0)) for (d, _), g in zip(chains, gch)]
    gcx = [t[0:GD_CHUNK] for t in cums]
    rest = [t[GD_CHUNK:2 * GD_CHUNK] for t in cums]
    tot = [t[2 * GD_CHUNK:3 * GD_CHUNK] for t in cums]
    dif = [_dot(lmat2_ref[d], jnp.concatenate(_split3(g * a), axis=0))
           for (d, _), g, a in zip(chains, gch, mask(2))]
    dec = each(lambda m, t: m * jnp.exp(t), mask(0), dif)
    egc = [jnp.exp(t) for t in gcx]
    kb = [t.astype(BF16) for t in kc]
    kq = each(lambda a, b: _dot_nt(jnp.concatenate([a, b.astype(BF16)], axis=0), _block_diag(a, bm_ref)), kb, qc)
    npk = each(lambda b, t, dd, m: b * t[0:GD_CHUNK] * dd * m, bx, kq, dec, mask(1))
    qk = each(lambda t, dd: t[GD_CHUNK:] * dd, kq, dec)
    n8 = [t * imask_ref[0] for t in npk]
    n8sq = each(mm3, n8, n8)
    pinv = each(lambda e, t: e - t, mask(3), n8)
    pinv = each(lambda p, m: p + mm3(p, m), pinv, n8sq)
    n8q4 = each(mm3, n8sq, n8sq)
    pinv = each(lambda p, m: p + mm3(p, m), pinv, n8q4)
    for lvl in range(1, 4):
        pc = each(lambda p, t: mm3(p, t * imask_ref[lvl]), pinv, npk)
        pinv = each(lambda p, t: p - mm3(t, p), pinv, pc)
    u = each(lambda p, a, b: mm3(p, a * b), pinv, vc, bx)
    w = each(lambda p, a, b, e: mm3(p, a * b * e), pinv, kc, bx, egc)
    for i, (d, c) in enumerate(chains):
        u_o[d, rows[i], :] = u[i]
        w_o[d, rows[i], :] = w[i].astype(BF16)
        qk_o[d, rows[i], :] = qk[i].astype(BF16)
        qd_o[d, rows[i], :] = (qc[i] * egc[i]).astype(BF16)
        kd_o[d, rows[i], :] = (kc[i] * jnp.exp(rest[i])).astype(BF16)
        eg_o[d, c * SUBLANES:(c + 1) * SUBLANES, :] = jnp.exp(tot[i][0:SUBLANES])


def _gdn_local(gdqkv, gdab, consts, nl_tiles):
    b, ntok, _ = gdqkv.shape
    nt = ntok // TM
    (cw, nea, dtb, eg, eb, lmat, lmat2, pmask, imask, gg64, bm256) = consts
    hpt = TM // HALO
    nhalo = ntok // HALO
    full = lambda a: pl.BlockSpec(a.shape, lambda i, t: (0,) * a.ndim)
    out = lambda rows: pl.BlockSpec((2, None, rows, 256), lambda i, t: (0, i, t, 0))
    kern = functools.partial(_gdn_local_kernel, nl_tiles=nl_tiles, nt_tiles=nt)
    nchunk = TM // GD_CHUNK
    return pl.pallas_call(
        kern,
        grid=(b, nt),
        in_specs=[pl.BlockSpec((None, TM, 768), lambda i, t: (i, t, 0)),
                  pl.BlockSpec((None, HALO, 768), lambda i, t: (i, jnp.maximum(t * hpt - 1, 0), 0)),
                  pl.BlockSpec((None, HALO, 768), lambda i, t: (i, jnp.minimum((t + 1) * hpt, nhalo - 1), 0)),
                  pl.BlockSpec((None, TM, LANES), lambda i, t: (i, t, 0)),
                  full(cw), full(nea), full(dtb), full(eg), full(eb), full(lmat), full(lmat2), full(pmask),
                  full(imask), full(gg64), full(bm256)],
        out_specs=[out(TM)] * 5 + [out(nchunk * SUBLANES)],
        out_shape=[jax.ShapeDtypeStruct((2, b, ntok, 256), F32)] +
                  [jax.ShapeDtypeStruct((2, b, ntok, 256), BF16)] * 4 +
                  [jax.ShapeDtypeStruct((2, b, nt * nchunk * SUBLANES, 256), F32)],
        scratch_shapes=[pltpu.VMEM((TM + 2 * HALO, 768), F32)],
        compiler_params=_params(("parallel", "parallel")),
        name="gdn_local",
    )(gdqkv, gdqkv, gdqkv, gdab, cw, nea, dtb, eg, eb, lmat, lmat2, pmask, imask, gg64, bm256)


def _gdn_scan_kernel(*refs):
    ins = (refs[0:6], refs[6:12])
    bm_ref, outs, s_ref = refs[12], refs[13:15], refs[15]
    j = pl.program_id(1)
    nchunk = TM // GD_CHUNK

    @pl.when(j == 0)
    def _init():
        s_ref[...] = jnp.zeros_like(s_ref)

    lane = lax.broadcasted_iota(I32, (1, 256), 1)
    s = [s_ref[0], s_ref[1]]
    dirs = range(2)
    for i in range(nchunk):
        cs = (i, nchunk - 1 - i)
        rows = [slice(c * GD_CHUNK, (c + 1) * GD_CHUNK) for c in cs]
        ld = lambda m: [ins[d][m][rows[d], :] for d in dirs]
        u, w, qk, qd, kd = ld(0), ld(1), ld(2), ld(3), ld(4)
        bds = [_block_diag(s[d].astype(BF16), bm_ref) for d in dirs]
        ws_qs = [_dot(jnp.concatenate([w[d], qd[d]], axis=0), bds[d]) for d in dirs]
        vb = [(u[d] - ws_qs[d][0:GD_CHUNK]).astype(BF16) for d in dirs]
        intra = [_dot(qk[d], _block_diag(vb[d], bm_ref)) for d in dirs]
        z = [_dot_tn(kd[d], vb[d]) for d in dirs]
        for d in dirs:
            outs[d][rows[d], :] = ws_qs[d][GD_CHUNK:] + intra[d]
            snew = s[d] * ins[d][5][cs[d] * SUBLANES:cs[d] * SUBLANES + 1, :]
            for h in range(GD_HEADS):
                snew = snew + jnp.where(lane // GD_HD == h, z[d][h * GD_HD:(h + 1) * GD_HD], 0.0)
            s[d] = snew
    s_ref[0] = s[0]
    s_ref[1] = s[1]


def _gdn_scan(u, w, qk, qd, kd, eg, bm256, nl_tiles):
    _, b, ntok, _ = u.shape
    nt = ntok // TM
    nc_tiles = nt - nl_tiles
    nchunk = TM // GD_CHUNK

    def tile_index(d, j):
        is_ctx = j < nc_tiles
        if d == 0:
            return jnp.where(is_ctx, nl_tiles + j, j - nc_tiles)
        return jnp.where(is_ctx, nl_tiles + nc_tiles - 1 - j, nl_tiles - 1 - (j - nc_tiles))

    tile = lambda d: pl.BlockSpec((None, None, TM, 256), lambda i, j: (d, i, tile_index(d, j), 0))
    egs = lambda d: pl.BlockSpec((None, None, nchunk * SUBLANES, 256), lambda i, j: (d, i, tile_index(d, j), 0))
    otile = lambda d: pl.BlockSpec((None, TM, 256), lambda i, j: (i, tile_index(d, j), 0))
    per_dir = lambda d: [tile(d)] * 5 + [egs(d)]
    return pl.pallas_call(
        _gdn_scan_kernel,
        grid=(b, nt),
        in_specs=per_dir(0) + per_dir(1) + [pl.BlockSpec(bm256.shape, lambda i, j: (0, 0))],
        out_specs=[otile(0), otile(1)],
        out_shape=[jax.ShapeDtypeStruct((b, ntok, 256), F32)] * 2,
        scratch_shapes=[pltpu.VMEM((2, GD_HD, 256), F32)],
        compiler_params=_params(("parallel", "arbitrary")),
        name="gdn_scan",
    )(u, w, qk, qd, kd, eg, u, w, qk, qd, kd, eg, bm256)


def _merge_kernel(x_ref, mod_ref, g1_ref, g2_ref, wg_ref, bg_ref, oa_ref, ob_ref, om_ref, of_ref, ob2_ref, gate_ref,
                  gdog_ref, gg_ref, wbr_ref, wo_ref, wrh_ref, wrl_ref, br_ref,
                  xn_o, h2_o, idx_o, gates_o):
    x = x_ref[...]
    mod = mod_ref[...]
    dm = D_MODEL
    sh1, sc1, gt1 = mod[:, 0:dm], mod[:, dm:2 * dm], mod[:, 2 * dm:3 * dm]
    sh2, sc2 = mod[:, 3 * dm:4 * dm], mod[:, 4 * dm:5 * dm]
    hb = (_rms_rows(x, g1_ref[...]) * (1.0 + sc1) + sh1).astype(BF16)

    o = of_ref[...] + ob2_ref[...]
    ss = _group_sum(o * o, gg_ref)
    od = (o * lax.rsqrt(ss * (1.0 / GD_HD) + EPS) * gdog_ref[...] * _silu(gate_ref[...])).astype(BF16)

    acc = jnp.zeros((TM, dm), F32)
    for n, br in enumerate((oa_ref[...], ob_ref[...], om_ref[...], od)):
        gpre = _dot(hb, wg_ref[:, n * dm:(n + 1) * dm]) + bg_ref[:, n * dm:(n + 1) * dm]
        acc = acc + _sigmoid(gpre) * _dot(br, wbr_ref[n])
    y = _dot(acc.astype(BF16), wo_ref[...])
    xn = x + gt1 * y
    xn_o[...] = xn
    h2 = _rms_rows(xn, g2_ref[...]) * (1.0 + sc2) + sh2
    for s in range(ROW_TILES):
        h2_o[pl.ds(s, TM, stride=ROW_TILES), :] = h2[:, s * LANES:(s + 1) * LANES]

    hh, hl = _split2(h2)
    logits = _dot(hh, wrh_ref[...]) + _dot(hl, wrh_ref[...]) + _dot(hh, wrl_ref[...]) + br_ref[...]
    lane = lax.broadcasted_iota(I32, logits.shape, 1)
    vals = jnp.full(logits.shape, NEG_BIG, F32)
    idxs = jnp.zeros(logits.shape, I32)
    l = logits
    for k in range(TOP_K):
        m = jnp.max(l, axis=-1, keepdims=True)
        i = jnp.min(jnp.where(l == m, lane, LANES), axis=-1, keepdims=True)
        vals = jnp.where(lane == k, m, vals)
        idxs = jnp.where(lane == k, i, idxs)
        l = jnp.where(lane == i, NEG_BIG * 2.0, l)
    e = jnp.exp(vals - jnp.max(vals, axis=-1, keepdims=True))
    e = jnp.where(lane < TOP_K, e, 0.0)
    gates_o[...] = e / jnp.sum(e, axis=-1, keepdims=True)
    idx_o[...] = idxs


def _merge(x_all, modsel, g1, g2, wg, bg, oa, ob, om, o2, gate, gdog, gg64, wbr, wo, wrh, wrl, brp, n_tiles, nl_tiles):
    b = x_all.shape[0]
    ntok_out = n_tiles * TM
    tok = lambda w: pl.BlockSpec((None, TM, w), lambda i, t: (i, t, 0))
    full = lambda a: pl.BlockSpec(a.shape, lambda i, t: (0,) * a.ndim)
    return pl.pallas_call(
        _merge_kernel,
        grid=(b, n_tiles),
        in_specs=[tok(D_MODEL),
                  pl.BlockSpec((None, None, 1, 6 * D_MODEL), lambda i, t: (i, t // nl_tiles, 0, 0)),
                  full(g1), full(g2), full(wg), full(bg), tok(256), tok(256), tok(256), tok(256), tok(256), tok(256),
                  full(gdog), full(gg64), full(wbr), full(wo), full(wrh), full(wrl), full(brp)],
        out_specs=[tok(D_MODEL), pl.BlockSpec((None, TM * ROW_TILES, LANES), lambda i, t: (i, t, 0)),
                   tok(LANES), tok(LANES)],
        out_shape=[jax.ShapeDtypeStruct((b, ntok_out, D_MODEL), F32),
                   jax.ShapeDtypeStruct((b, ntok_out * ROW_TILES, LANES), F32),
                   jax.ShapeDtypeStruct((b, ntok_out, LANES), I32), jax.ShapeDtypeStruct((b, ntok_out, LANES), F32)],
        compiler_params=_params(("parallel", "parallel")),
        name="merge",
    )(x_all, modsel, g1, g2, wg, bg, oa, ob, om, o2[0], o2[1], gate, gdog, gg64, wbr, wo, wrh, wrl, brp)


def _expert_kernel(bexp_ref, src0_ref, src_ref, dst_ref, tok_hbm, wup_ref, bup_ref, wdn_ref, bdn_ref, out_hbm,
                   x0, x1, x2, x3, y0, y1, y2, y3, gsem, ssem):
    s = pl.program_id(0)
    ns = pl.num_programs(0)
    xs = (x0, x1, x2, x3)
    ys = (y0, y1, y2, y3)

    def tile_rows(first):
        return pl.ds(pl.multiple_of(first, ROW_TILES), ROW_TILES)

    def issue_gather(idx_ref, blk, k):
        for r in range(EXP_SUB):
            pltpu.make_async_copy(tok_hbm.at[tile_rows(idx_ref[blk, r]), :], xs[k].at[pl.ds(r * ROW_TILES, ROW_TILES), :],
                                  gsem.at[k]).start(priority=r % DMA_PRIORITIES)

    def wait_gather(k):
        pltpu.make_async_copy(tok_hbm.at[pl.ds(0, EXP_SUB * ROW_TILES), :], xs[k], gsem.at[k]).wait()

    def issue_scatter(k):
        for r in range(EXP_SUB):
            pltpu.make_async_copy(ys[k].at[pl.ds(r * ROW_TILES, ROW_TILES), :], out_hbm.at[tile_rows(dst_ref[k, r]), :],
                                  ssem.at[k]).start(priority=r % DMA_PRIORITIES)

    def wait_scatter(k):
        pltpu.make_async_copy(ys[k], out_hbm.at[pl.ds(0, EXP_SUB * ROW_TILES), :], ssem.at[k]).wait()

    @pl.when(s == 0)
    def _prologue():
        issue_gather(src0_ref, 0, 0)
        issue_gather(src0_ref, 1, 1)

    for k in range(EXP_NSUB):
        wait_gather(k)

        @pl.when(s > 0)
        def _free_y():
            wait_scatter(k)

        if k >= 1:
            issue_scatter(k - 1)
        issue_gather(src_ref, k, (k + 2) % EXP_NSUB)
        xb = jnp.concatenate([xs[k][pl.ds(t, EXP_SUB, stride=ROW_TILES), :].astype(BF16) for t in range(ROW_TILES)],
                             axis=1)
        z = _dot(xb, wup_ref[...]) + bup_ref[...]
        glu = jnp.minimum(z[:, :D_FF], SWIGLU_LIMIT)
        lin = jnp.clip(z[:, D_FF:], -SWIGLU_LIMIT, SWIGLU_LIMIT)
        act = glu * _sigmoid(SWIGLU_ALPHA * glu) * (lin + 1.0)
        y = _dot(act.astype(BF16), wdn_ref[...]) + bdn_ref[...]
        for t in range(ROW_TILES):
            ys[k][pl.ds(t, EXP_SUB, stride=ROW_TILES), :] = y[:, t * LANES:(t + 1) * LANES]
    issue_scatter(EXP_NSUB - 1)

    @pl.when(s == ns - 1)
    def _epilogue():
        for k in range(EXP_NSUB):
            wait_scatter(k)
        wait_gather(0)
        wait_gather(1)


def _experts(blk_exp, row_src, row_dst, tok, wup, bup, wdn, bdn):
    ns = blk_exp.shape[0]
    n_rows = ns * EXP_TILE
    src = jnp.concatenate([row_src, jnp.zeros((2 * EXP_SUB,), I32)])
    src0 = src[:2 * EXP_SUB].reshape(2, EXP_SUB)
    src_ahead = jnp.concatenate([src[2 * EXP_SUB:], jnp.zeros((2 * EXP_SUB,), I32)])[:n_rows]
    idx = lambda: pl.BlockSpec((None, EXP_NSUB, EXP_SUB), lambda i, be: (i, 0, 0), memory_space=pltpu.SMEM)
    gs = pltpu.PrefetchScalarGridSpec(
        num_scalar_prefetch=1,
        grid=(ns,),
        in_specs=[pl.BlockSpec(memory_space=pltpu.SMEM),
                  idx(), idx(),
                  pl.BlockSpec(memory_space=pl.ANY),
                  pl.BlockSpec((None, D_MODEL, 2 * D_FF), lambda i, be: (be[i], 0, 0)),
                  pl.BlockSpec((None, 1, 2 * D_FF), lambda i, be: (be[i], 0, 0)),
                  pl.BlockSpec((None, D_FF, D_MODEL), lambda i, be: (be[i], 0, 0)),
                  pl.BlockSpec((None, 1, D_MODEL), lambda i, be: (be[i], 0, 0))],
        out_specs=pl.BlockSpec(memory_space=pl.ANY),
        scratch_shapes=[pltpu.VMEM((EXP_SUB * ROW_TILES, LANES), F32)] * (2 * EXP_NSUB) +
                       [pltpu.SemaphoreType.DMA((EXP_NSUB,)), pltpu.SemaphoreType.DMA((EXP_NSUB,))],
    )
    return pl.pallas_call(
        _expert_kernel,
        grid_spec=gs,
        out_shape=jax.ShapeDtypeStruct((n_rows * ROW_TILES, LANES), F32),
        compiler_params=_params(("arbitrary",)),
        name="experts",
    )(blk_exp, src0, src_ahead.reshape(ns, EXP_NSUB, EXP_SUB), row_dst.reshape(ns, EXP_NSUB, EXP_SUB),
      tok, wup, bup, wdn, bdn)


def _combine_kernel(x_ref, mod_ref, gates_ref, y0_ref, y1_ref, y2_ref, y3_ref, o_ref):
    g2 = mod_ref[...][:, 5 * D_MODEL:6 * D_MODEL]
    gates = gates_ref[...]
    for s in range(ROW_TILES):
        cols = slice(s * LANES, (s + 1) * LANES)
        f = jnp.zeros((TM, LANES), F32)
        for k, y_ref in enumerate((y0_ref, y1_ref, y2_ref, y3_ref)):
            f = f + gates[:, k:k + 1] * y_ref[pl.ds(s, TM, stride=ROW_TILES), :]
        o_ref[:, cols] = x_ref[:, cols] + g2[:, cols] * f


def _combine(xn, modsel, gates, out4, nl_tiles):
    b, ntok, _ = xn.shape
    nt = ntok // TM
    slab = b * nt
    tok = lambda w: pl.BlockSpec((None, TM, w), lambda i, t: (i, t, 0))
    ys = lambda k: pl.BlockSpec((TM * ROW_TILES, LANES), lambda i, t: (k * slab + i * nt + t, 0))
    return pl.pallas_call(
        _combine_kernel,
        grid=(b, nt),
        in_specs=[tok(D_MODEL),
                  pl.BlockSpec((None, None, 1, 6 * D_MODEL), lambda i, t: (i, t // nl_tiles, 0, 0)),
                  tok(LANES), ys(0), ys(1), ys(2), ys(3)],
        out_specs=tok(D_MODEL),
        out_shape=jax.ShapeDtypeStruct((b, ntok, D_MODEL), F32),
        compiler_params=_params(("parallel", "parallel")),
        name="combine",
    )(xn, modsel, gates, out4, out4, out4, out4)


def _np_consts():
    lane = np.arange(256)
    half, r = lane // 128, lane % 128
    s, j = r // 16, r % 16
    d = np.where(j < 8, j + 8 * half, 16 + (j - 8) + 8 * half)
    perm_da = s * 32 + d
    dperm_da = d
    h, j = r // 32, r % 32
    d = np.where(j < 16, j + 16 * half, 32 + (j - 16) + 16 * half)
    perm_gq = h * 64 + d
    perm_gk = (h // 2) * 64 + d
    dperm_ga = d
    hv, dv = lane // 64, lane % 64
    perm_gv = (hv // 2) * 64 + dv
    grp = lambda key: (key[:, None] == key[None, :]).astype(np.float32)
    gda = grp((lane % 128) // 16)
    gga = grp((lane % 128) // 32)
    g64 = grp(lane // 64)
    stack2 = lambda g: np.concatenate([g, g], axis=0)
    bm256 = (np.arange(256)[:, None] // 64 == lane[None, :] // 64).astype(np.float32)
    bm512 = (np.arange(512)[:, None] // 128 == lane[None, :] // 64).astype(np.float32)
    i = np.arange(64)
    jj = lane % 64
    lmat, lmat2, pmask = [], [], []
    for dirn in range(2):
        if dirn == 0:
            inc = (i[None, :] <= i[:, None]); aft = (i[None, :] > i[:, None])
            p_incl = (i[:, None] >= jj[None, :]); p_strict = (i[:, None] > jj[None, :]); p_after = (i[:, None] > jj[None, :])
        else:
            inc = (i[None, :] >= i[:, None]); aft = (i[None, :] < i[:, None])
            p_incl = (i[:, None] <= jj[None, :]); p_strict = (i[:, None] < jj[None, :]); p_after = (i[:, None] < jj[None, :])
        ones = np.ones((64, 64))
        l1 = np.concatenate([inc, aft, ones], axis=0).astype(np.float32)
        lmat.append(np.concatenate([l1, l1, l1], axis=1))
        l2 = inc.astype(np.float32)
        lmat2.append(np.concatenate([l2, l2, l2], axis=1))
        eye = (i[:, None] == jj[None, :])
        pmask.append(np.stack([p_incl, p_strict, p_after, eye]).astype(np.float32))
    bi, bj = i[:, None], jj[None, :]
    imask = np.stack([bi // 8 == bj // 8,
                      (bi // 16 == bj // 16) & (bi // 8 != bj // 8),
                      (bi // 32 == bj // 32) & (bi // 16 != bj // 16),
                      bi // 32 != bj // 32]).astype(np.float32)
    eg = np.zeros((2, 128, 256), np.float32)
    eb = np.zeros((2, 128, 256), np.float32)
    for dirn in range(2):
        for hh in range(GD_HEADS):
            eg[dirn, dirn * GD_HEADS + hh, hh * 64:(hh + 1) * 64] = 1.0
            eb[dirn, 2 * GD_HEADS + dirn * GD_HEADS + hh, hh * 64:(hh + 1) * 64] = 1.0
    eg = np.concatenate([eg, eg, eg], axis=1)
    eb = np.concatenate([eb, eb, eb], axis=1)
    return dict(perm_da=perm_da, dperm_da=dperm_da, perm_gq=perm_gq, perm_gk=perm_gk, dperm_ga=dperm_ga, perm_gv=perm_gv,
                gda=stack2(gda), gga=stack2(gga), g64=stack2(g64), bm256=bm256, bm512=bm512,
                lmat=np.stack(lmat), lmat2=np.stack(lmat2), pmask=np.stack(pmask), imask=imask, eg=eg, eb=eb)


def _rope_tables(seq, ctx_len, quarter, reps):
    t = jnp.arange(seq)
    row = (t // GRID_W).astype(F32)
    col = (t % GRID_W).astype(F32)
    inv = ROPE_THETA ** (-jnp.arange(quarter, dtype=F32) / quarter)
    ang = jnp.concatenate([row[:, None] * inv, col[:, None] * inv], axis=1)
    ang = jnp.tile(ang, (1, reps))
    cos = jnp.concatenate([jnp.cos(ang), jnp.ones((ctx_len, LANES), F32)], axis=0)
    sin = jnp.concatenate([jnp.sin(ang), jnp.zeros((ctx_len, LANES), F32)], axis=0)
    return cos, sin


def _routing(idx, n_tok):
    n_asg = n_tok * TOP_K
    e_flat = idx.reshape(-1)
    order = jnp.argsort(e_flat, stable=True).astype(I32)
    counts = jnp.sum((e_flat[:, None] == jnp.arange(N_EXPERTS, dtype=I32)[None, :]).astype(I32), axis=0)
    padded = (counts + EXP_TILE - 1) // EXP_TILE * EXP_TILE
    start = jnp.cumsum(counts) - counts
    pend = jnp.cumsum(padded)
    pstart = pend - padded
    n_tiles = (n_asg + N_EXPERTS * (EXP_TILE - 1)) // EXP_TILE + 1
    n_rows = n_tiles * EXP_TILE
    tile_row0 = jnp.arange(n_tiles, dtype=I32) * EXP_TILE
    tile_exp = jnp.minimum(jnp.sum((pend[None, :] <= tile_row0[:, None]).astype(I32), axis=1), N_EXPERTS - 1)
    row = jnp.arange(n_rows, dtype=I32)
    e_row = jnp.repeat(tile_exp, EXP_TILE)
    local = row - pstart[e_row]
    valid = local < counts[e_row]
    asg = order[jnp.minimum(start[e_row] + local, n_asg - 1)]
    tok_of = asg // TOP_K
    row_src = jnp.where(valid, tok_of, 0)
    n_pad_before = row - (start[e_row] + counts[e_row])
    row_dst = jnp.where(valid, (asg % TOP_K) * n_tok + tok_of, n_asg + n_pad_before)
    return tile_exp.astype(I32), (row_src * ROW_TILES).astype(I32), (row_dst * ROW_TILES).astype(I32)


def kernel(x, c, ctx, c_ctx, w_mod, b_mod, norm1_g, norm2_g, w_in, b_gate, da_q_g, da_k_g, da_lam_q1, da_lam_k1,
           da_lam_q2, da_lam_k2, da_sub_g, ga_q_g, ga_k_g, gm_v_g, gm_ws, gm_bs, gd_conv_w, gd_a_log, gd_dt_bias,
           gd_out_g, w_br, w_o, w_router, b_router, w_up, b_up, w_down, b_down):
    bsz, seq, dm = x.shape
    ctx_len = ctx.shape[1]
    depth = w_mod.shape[0]
    assert dm == D_MODEL and seq % TM == 0 and ctx_len % TM == 0 and seq % ctx_len == 0 and seq % GRID_W == 0
    ntok = seq + ctx_len
    nl_tiles = seq // TM
    nt_tiles = ntok // TM
    cn = _np_consts()
    bfc = lambda a: jnp.asarray(a, BF16)
    gda, gga, g64 = bfc(cn['gda']), bfc(cn['gga']), bfc(cn['g64'])
    bm256, bm512 = bfc(cn['bm256']), bfc(cn['bm512'])
    lmat, lmat2 = bfc(cn['lmat']), bfc(cn['lmat2'])
    pmask = jnp.asarray(cn['pmask'], F32)
    imask = jnp.asarray(cn['imask'], F32)
    eg, eb = bfc(cn['eg']), bfc(cn['eb'])
    cosa, sina = _rope_tables(seq, ctx_len, DA_HD // 4, LANES // (DA_HD // 2))
    cosg, sing = _rope_tables(seq, ctx_len, GA_HD // 4, LANES // (GA_HD // 2))
    cuts = np.cumsum((256, 256, 256, 256, 128, 128, 256, 256, 768, 256, 8, 8))
    mrows = -(-(bsz + 1) // SUBLANES) * SUBLANES
    cpad = jnp.zeros((mrows, dm), F32).at[:bsz].set(c).at[bsz].set(c_ctx)

    x_all = jnp.concatenate([x, ctx], axis=1)
    for l in range(depth):
        need_ctx = l < depth - 1
        lam_init = 0.8 - 0.6 * math.exp(-0.3 * l)
        mod = _adaln(cpad, w_mod[l], b_mod[l])
        modsel = jnp.stack([mod[:bsz], jnp.broadcast_to(mod[bsz], (bsz, 6 * dm))], axis=1).reshape(bsz, 2, 1, 6 * dm)

        wl = w_in[l]
        aq, ak, av, gq, gk, gv, mu, mv, dqkv, dgate, dda, ddb = jnp.split(wl[:, :cuts[-1]], cuts[:-1].tolist(), axis=1)
        wcat = jnp.concatenate([aq[:, cn['perm_da']], ak[:, cn['perm_da']], av, gq[:, cn['perm_gq']],
                                gk[:, cn['perm_gk']], gv[:, cn['perm_gv']], mu, mv, dqkv, dgate, dda, ddb,
                                jnp.zeros((dm, LANES - 4 * GD_HEADS), F32)], axis=1).astype(BF16)
        row = lambda a: a.reshape(1, -1).astype(F32)
        daqg = row(da_q_g[l][cn['dperm_da']])
        dakg = row(da_k_g[l][cn['dperm_da']])
        gaqg = row(ga_q_g[l][cn['dperm_ga']])
        gakg = row(ga_k_g[l][cn['dperm_ga']])
        wsp = gm_ws[l].transpose(1, 0, 2).reshape(GM_CHUNK, GM_GROUPS * GM_CHUNK).astype(BF16)
        bsp = jnp.repeat(gm_bs[l].T, GM_GW, axis=1)
        consts = (gda, gga, cosa, sina, cosg, sing, daqg, dakg, gaqg, gakg, row(gm_v_g[l]), wsp, bsp, bm512)
        daq, dak, dav, gaq, gak, gav, om, gdqkv, gdgate, gdab = _inproj(
            x_all, modsel, row(norm1_g[l]), wcat, consts, nl_tiles)

        lam = (jnp.exp(jnp.sum(da_lam_q1[l] * da_lam_k1[l])) - jnp.exp(jnp.sum(da_lam_q2[l] * da_lam_k2[l]))
               + lam_init).reshape(1).astype(F32)
        subg = row(jnp.tile(da_sub_g[l], DA_HEADS))
        att = functools.partial(_attention, lam, gg64=g64, subg=subg)
        lat = dict(q_start=0, q_len=seq, kv_start=0, kv_len=ntok)
        oa = att(daq, dak, dav, diff=True, post_scale=1.0 - lam_init, **lat)
        ob = att(gaq, gak, gav, diff=False, post_scale=1.0, **lat)
        if need_ctx:
            cx = dict(q_start=seq, q_len=ctx_len, kv_start=seq, kv_len=ctx_len)
            oa = jnp.concatenate([oa, att(daq, dak, dav, diff=True, post_scale=1.0 - lam_init, **cx)], axis=1)
            ob = jnp.concatenate([ob, att(gaq, gak, gav, diff=False, post_scale=1.0, **cx)], axis=1)

        nea = jnp.zeros((1, LANES), F32).at[0, :2 * GD_HEADS].set(-jnp.exp(gd_a_log[l].reshape(-1)))
        dtb = jnp.zeros((1, LANES), F32).at[0, :2 * GD_HEADS].set(gd_dt_bias[l].reshape(-1))
        gconsts = (gd_conv_w[l], nea, dtb, eg, eb, lmat, lmat2, pmask, imask, g64, bm256)
        u, w, qk, qd, kd, egl = _gdn_local(gdqkv, gdab, gconsts, nl_tiles)
        o2 = _gdn_scan(u, w, qk, qd, kd, egl, bm256, nl_tiles)

        n_tiles = nt_tiles if need_ctx else nl_tiles
        wg = wl[:, cuts[-1]:].astype(BF16)
        wr = jnp.zeros((dm, LANES), F32).at[:, :N_EXPERTS].set(w_router[l])
        wrh = wr.astype(BF16)
        wrl = (wr - wrh.astype(F32)).astype(BF16)
        brp = jnp.full((1, LANES), NEG_BIG, F32).at[0, :N_EXPERTS].set(b_router[l])
        xn, h2, idx, gates = _merge(x_all, modsel, row(norm1_g[l]), row(norm2_g[l]), wg, row(b_gate[l]),
                                    oa, ob, om, o2, gdgate, row(jnp.tile(gd_out_g[l], GD_HEADS)), g64,
                                    w_br[l].astype(BF16), w_o[l].astype(BF16), wrh, wrl, brp, n_tiles, nl_tiles)

        n_tok = bsz * n_tiles * TM
        tile_exp, row_src, row_dst = _routing(idx[:, :, :TOP_K], n_tok)
        out4 = _experts(tile_exp, row_src, row_dst, h2.reshape(n_tok * ROW_TILES, LANES), w_up[l].astype(BF16),
                        b_up[l].reshape(N_EXPERTS, 1, 2 * D_FF), w_down[l].astype(BF16),
                        b_down[l].reshape(N_EXPERTS, 1, dm))
        x_all = _combine(xn, modsel, gates, out4, nl_tiles)
    return x_all[:, :seq] if x_all.shape[1] != seq else x_all
```
